```python
import math
import jax
import jax.numpy as jnp
from jax import lax
import numpy as np

D_MODEL = 2048
BATCH = 2
SEQ = 4096
DEPTH = 4

HEAD_DIM = 128
N_DIFF_HEADS = D_MODEL // (2 * HEAD_DIM)
DIFF_QK_DIM = HEAD_DIM // 2
N_NSA_HEADS = D_MODEL // (2 * HEAD_DIM)
N_NSA_KV = 2
NSA_HPG = N_NSA_HEADS // N_NSA_KV
NSA_CMP_LEN = 32
NSA_CMP_STRIDE = 16
NSA_CMP_HIDDEN = 256
NSA_SEL_BLOCK = 64
NSA_TOP_N = 16
NSA_WINDOW = 512
D_FF = 5632
ROPE_THETA = 500000.0
ROPE_FRACTION = 4
Q_BLOCK = 128
SEL_Q_BLOCK = 64
N_MOD = 9
RMS_EPS = 1e-6
NEG = -1e30
FORCE_BONUS = 1e6
DIFF_W = N_DIFF_HEADS * HEAD_DIM
NSA_W = N_NSA_HEADS * HEAD_DIM
NSA_KV_W = N_NSA_KV * HEAD_DIM
IN_COLS = 3 * DIFF_W + NSA_W + 6 * NSA_KV_W + 3 * N_NSA_HEADS

kernel_name = 'hybrid_diffattn_nsa_macaron_adaln'


def rms_norm(x, g):
    xf = x.astype(jnp.float32)
    y = xf * lax.rsqrt(jnp.mean(xf * xf, axis=-1, keepdims=True) + RMS_EPS)
    return (y * g.astype(jnp.float32)).astype(x.dtype)


def rope(x, rot_dim):
    S = x.shape[1]
    half = rot_dim // 2
    inv = jnp.exp(-math.log(ROPE_THETA) * jnp.arange(half, dtype=jnp.float32) / half)
    ang = jnp.arange(S, dtype=jnp.float32)[:, None] * inv[None, :]
    shp = (S,) + (1,) * (x.ndim - 3) + (half,)
    cos = jnp.cos(ang).reshape(shp).astype(x.dtype)
    sin = jnp.sin(ang).reshape(shp).astype(x.dtype)
    x1 = x[..., :half]
    x2 = x[..., half:rot_dim]
    return jnp.concatenate([x1 * cos - x2 * sin, x2 * cos + x1 * sin, x[..., rot_dim:]], axis=-1)


def swiglu(h, w_gu, w_d):
    g, u = jnp.split(h @ w_gu, 2, axis=-1)
    return (jax.nn.silu(g) * u) @ w_d


def gather_rows(a, idx):
    return jax.vmap(jax.vmap(lambda ai, ii: ai[ii]))(a, idx)


def diff_attention(q, k, v, lam, subln_g, lam_init):
    B, S, H = q.shape[0], q.shape[1], q.shape[2]
    nb = S // Q_BLOCK
    scale = DIFF_QK_DIM ** -0.5
    kpos = jnp.arange(S)
    vf = v.astype(jnp.float32)

    def block(i):
        t0 = i * Q_BLOCK
        qb = lax.dynamic_slice_in_dim(q, t0, Q_BLOCK, axis=1)
        s = jnp.einsum('bqhcd,bkhcd->bhcqk', qb, k).astype(jnp.float32) * scale
        mask = (t0 + jnp.arange(Q_BLOCK))[:, None] >= kpos[None, :]
        p = jax.nn.softmax(jnp.where(mask, s, NEG), axis=-1)
        a = p[:, :, 0] - lam * p[:, :, 1]
        return jnp.einsum('bhqk,bkhd->bqhd', a, vf)

    o = lax.map(block, jnp.arange(nb))
    o = o.transpose(1, 0, 2, 3, 4).reshape(B, S, H, HEAD_DIM)
    o = rms_norm(o, subln_g) * (1.0 - lam_init)
    return o.astype(v.dtype)


def nsa_attention(q, k_c, v_c, k_s, v_s, k_w, v_w, gates, cmp_pe, cmp_w1, cmp_w2):
    B, S, H, dk = q.shape
    G, J = N_NSA_KV, NSA_HPG
    qg = q.reshape(B, S, G, J, dk)
    scale = dk ** -0.5
    tpos = jnp.arange(S)

    nc = (S - NSA_CMP_LEN) // NSA_CMP_STRIDE + 1
    cidx = jnp.arange(nc)[:, None] * NSA_CMP_STRIDE + jnp.arange(NSA_CMP_LEN)[None, :]

    def compress(kv, j):
        blk = kv[:, cidx] + cmp_pe[j][None, None, :, None, :]
        blk = blk.transpose(0, 1, 3, 2, 4).reshape(B, nc, G, NSA_CMP_LEN * dk)
        return jax.nn.silu(blk @ cmp_w1[j]) @ cmp_w2[j]

    kc = compress(k_c, 0)
    vc = compress(v_c, 1)
    s_c = jnp.einsum('bsgjd,bngd->bgjsn', qg, kc).astype(jnp.float32) * scale
    cend = jnp.arange(nc) * NSA_CMP_STRIDE + NSA_CMP_LEN - 1
    cmask = cend[None, :] <= tpos[:, None]
    p_c = jnp.where(cmask, jax.nn.softmax(jnp.where(cmask, s_c, NEG), axis=-1), 0.0)
    o_c = jnp.einsum('bgjsn,bngd->bsgjd', p_c, vc.astype(jnp.float32))

    ns = S // NSA_SEL_BLOCK
    n_top = min(NSA_TOP_N, ns)
    cs = jnp.arange(nc) * NSA_CMP_STRIDE
    ss = jnp.arange(ns) * NSA_SEL_BLOCK
    ov = jnp.clip(jnp.minimum(cs[:, None] + NSA_CMP_LEN, ss[None, :] + NSA_SEL_BLOCK)
                  - jnp.maximum(cs[:, None], ss[None, :]), 0, None).astype(jnp.float32) / NSA_CMP_LEN
    imp = jnp.einsum('bgjsn,nm->bgsm', p_c, ov)
    sblk = jnp.arange(ns)[None, :]
    cur = (tpos // NSA_SEL_BLOCK)[:, None]
    valid = sblk * NSA_SEL_BLOCK <= tpos[:, None]
    forced = (sblk == 0) | (sblk == cur) | (sblk == cur - 1)
    score = jnp.where(valid, imp + jnp.where(forced, FORCE_BONUS, 0.0), NEG)
    top_val, top_idx = lax.top_k(score, n_top)
    top_ok = top_val > 0.5 * NEG

    kv_blk = jnp.concatenate([k_s, v_s], axis=-1).transpose(0, 2, 1, 3)
    kv_blk = kv_blk.reshape(B, G, ns, NSA_SEL_BLOCK, 2 * dk)

    def sel_block(i):
        t0 = i * SEL_Q_BLOCK
        qb = lax.dynamic_slice_in_dim(qg, t0, SEL_Q_BLOCK, axis=1)
        ib = lax.dynamic_slice_in_dim(top_idx, t0, SEL_Q_BLOCK, axis=2)
        okb = lax.dynamic_slice_in_dim(top_ok, t0, SEL_Q_BLOCK, axis=2)
        kvg = gather_rows(kv_blk, ib.reshape(B, G, SEL_Q_BLOCK * n_top))
        kvg = kvg.reshape(B, G, SEL_Q_BLOCK, n_top * NSA_SEL_BLOCK, 2 * dk)
        kg = kvg[..., :dk]
        vg = kvg[..., dk:].astype(jnp.float32)
        s = jnp.einsum('bqgjd,bgqkd->bgjqk', qb, kg).astype(jnp.float32) * scale
        tok = (ib[..., None] * NSA_SEL_BLOCK + jnp.arange(NSA_SEL_BLOCK)).reshape(B, G, SEL_Q_BLOCK, -1)
        tq = t0 + jnp.arange(SEL_Q_BLOCK)
        m = (tok <= tq[:, None]) & jnp.repeat(okb, NSA_SEL_BLOCK, axis=-1)
        p = jax.nn.softmax(jnp.where(m[:, :, None], s, NEG), axis=-1)
        return jnp.einsum('bgjqk,bgqkd->bqgjd', p, vg)

    o_s = lax.map(sel_block, jnp.arange(S // SEL_Q_BLOCK))
    o_s = o_s.transpose(1, 0, 2, 3, 4, 5).reshape(B, S, G, J, dk)

    nb = S // Q_BLOCK
    span = NSA_WINDOW + Q_BLOCK
    widx = jnp.arange(nb)[:, None] * Q_BLOCK + jnp.arange(span)[None, :]
    pad = ((0, 0), (NSA_WINDOW, 0), (0, 0), (0, 0))
    kb = jnp.pad(k_w, pad)[:, widx]
    vb = jnp.pad(v_w, pad)[:, widx].astype(jnp.float32)
    qb = qg.reshape(B, nb, Q_BLOCK, G, J, dk)
    s_w = jnp.einsum('bnqgjd,bnkgd->bngjqk', qb, kb).astype(jnp.float32) * scale
    kpos = widx - NSA_WINDOW
    qpos = jnp.arange(nb)[:, None] * Q_BLOCK + jnp.arange(Q_BLOCK)[None, :]
    dist = qpos[:, :, None] - kpos[:, None, :]
    wmask = (dist >= 0) & (dist < NSA_WINDOW) & (kpos[:, None, :] >= 0)
    p_w = jax.nn.softmax(jnp.where(wmask[None, :, None, None], s_w, NEG), axis=-1)
    o_w = jnp.einsum('bngjqk,bnkgd->bnqgjd', p_w, vb).reshape(B, S, G, J, dk)

    g = gates.reshape(B, S, G, J, 3)
    o = g[..., 0:1] * o_c + g[..., 1:2] * o_s + g[..., 2:3] * o_w
    return o.reshape(B, S, H * dk)


def token_mixer(h, w_in, w_o, diff_lam, diff_subln, cmp_pe, cmp_w1, cmp_w2, lam_init):
    B, S, _ = h.shape
    proj = h @ w_in
    sizes = [DIFF_W, DIFF_W, DIFF_W, NSA_W] + [NSA_KV_W] * 6 + [3 * N_NSA_HEADS]
    offs = []
    acc = 0
    for sz in sizes[:-1]:
        acc += sz
        offs.append(acc)
    dq, dk_, dv, nq, kc, vc, ks, vs, kw, vw, gl = jnp.split(proj, offs, axis=-1)

    dq = rope(dq.reshape(B, S, N_DIFF_HEADS, 2, DIFF_QK_DIM), DIFF_QK_DIM // ROPE_FRACTION)
    dk_ = rope(dk_.reshape(B, S, N_DIFF_HEADS, 2, DIFF_QK_DIM), DIFF_QK_DIM // ROPE_FRACTION)
    dv = dv.reshape(B, S, N_DIFF_HEADS, HEAD_DIM)
    lf = diff_lam.astype(jnp.float32)
    lam = jnp.exp(jnp.sum(lf[0] * lf[1])) - jnp.exp(jnp.sum(lf[2] * lf[3])) + lam_init
    o_diff = diff_attention(dq, dk_, dv, lam, diff_subln, lam_init)

    kvs = (B, S, N_NSA_KV, HEAD_DIM)
    rot = HEAD_DIM // ROPE_FRACTION
    nq = rope(nq.reshape(B, S, N_NSA_HEADS, HEAD_DIM), rot)
    ks = rope(ks.reshape(kvs), rot)
    kw = rope(kw.reshape(kvs), rot)
    gates = jax.nn.sigmoid(gl.reshape(B, S, N_NSA_HEADS, 3))
    o_nsa = nsa_attention(nq, kc.reshape(kvs), vc.reshape(kvs), ks, vs.reshape(kvs),
                          kw, vw.reshape(kvs), gates, cmp_pe, cmp_w1, cmp_w2)

    o = jnp.concatenate([o_diff.reshape(B, S, DIFF_W), o_nsa.astype(h.dtype)], axis=-1)
    return o @ w_o


def setup_inputs(seed: int = 0) -> dict:
    key = jax.random.key(seed)
    ks = jax.random.split(key, 15)
    f32 = jnp.float32
    D = D_MODEL
    cl_dk = NSA_CMP_LEN * HEAD_DIM
    return {
        'x': jax.random.normal(ks[0], (BATCH, SEQ, D), f32),
        'c': jax.random.normal(ks[1], (BATCH, D), f32),
        'w_ada': jax.random.normal(ks[2], (DEPTH, D, N_MOD * D), f32) * (0.5 * D ** -0.5),
        'b_ada': jax.random.normal(ks[3], (DEPTH, N_MOD * D), f32) * 0.02,
        'norm_g': 1.0 + 0.02 * jax.random.normal(ks[4], (DEPTH, 3, D), f32),
        'ffn_w_gu': jax.random.normal(ks[5], (DEPTH, 2, D, 2 * D_FF), f32) * D ** -0.5,
        'ffn_w_d': jax.random.normal(ks[6], (DEPTH, 2, D_FF, D), f32) * D_FF ** -0.5,
        'w_in': jax.random.normal(ks[7], (DEPTH, D, IN_COLS), f32) * D ** -0.5,
        'w_o': jax.random.normal(ks[8], (DEPTH, D, D), f32) * D ** -0.5,
        'diff_lam': jax.random.normal(ks[9], (DEPTH, 4, DIFF_QK_DIM), f32) * 0.1,
        'diff_subln': 1.0 + 0.02 * jax.random.normal(ks[10], (DEPTH, HEAD_DIM), f32),
        'cmp_pe': jax.random.normal(ks[11], (DEPTH, 2, NSA_CMP_LEN, HEAD_DIM), f32) * 0.1,
        'cmp_w1': jax.random.normal(ks[12], (DEPTH, 2, cl_dk, NSA_CMP_HIDDEN), f32) * cl_dk ** -0.5,
        'cmp_w2': jax.random.normal(ks[13], (DEPTH, 2, NSA_CMP_HIDDEN, HEAD_DIM), f32) * NSA_CMP_HIDDEN ** -0.5,
        'final_g': 1.0 + 0.02 * jax.random.normal(ks[14], (D,), f32),
    }


def reference(x, c, w_ada, b_ada, norm_g, ffn_w_gu, ffn_w_d, w_in, w_o, diff_lam,
              diff_subln, cmp_pe, cmp_w1, cmp_w2, final_g):
    B = x.shape[0]
    D = x.shape[-1]
    cs = jax.nn.silu(c)
    for l in range(DEPTH):
        mod = (cs @ w_ada[l] + b_ada[l]).reshape(B, N_MOD, 1, D)
        lam_init = 0.8 - 0.6 * math.exp(-0.3 * l)
        h = rms_norm(x, norm_g[l, 0]) * (1.0 + mod[:, 1]) + mod[:, 0]
        x = x + 0.5 * mod[:, 2] * swiglu(h, ffn_w_gu[l, 0], ffn_w_d[l, 0])
        h = rms_norm(x, norm_g[l, 1]) * (1.0 + mod[:, 4]) + mod[:, 3]
        x = x + mod[:, 5] * token_mixer(h, w_in[l], w_o[l], diff_lam[l], diff_subln[l],
                                        cmp_pe[l], cmp_w1[l], cmp_w2[l], lam_init)
        h = rms_norm(x, norm_g[l, 2]) * (1.0 + mod[:, 7]) + mod[:, 6]
        x = x + 0.5 * mod[:, 8] * swiglu(h, ffn_w_gu[l, 1], ffn_w_d[l, 1])
    return rms_norm(x, final_g)
```

```python
import functools
import math

import jax
import jax.numpy as jnp
from jax import lax
from jax.experimental import pallas as pl
from jax.experimental.pallas import tpu as pltpu

F32 = jnp.float32
BF16 = jnp.bfloat16

HEAD_DIM = 128
DIFF_QK_DIM = HEAD_DIM // 2
N_NSA_KV = 2
NSA_CMP_LEN = 32
NSA_CMP_STRIDE = 16
NSA_SEL_BLOCK = 64
NSA_TOP_N = 16
NSA_WINDOW = 512
ROPE_THETA = 500000.0
ROPE_FRACTION = 4
N_MOD = 9
RMS_EPS = 1e-6
NEG = -1e30
FORCE_BONUS = 1e6

V7X_LANES = 128
V7X_VMEM_LIMIT_BYTES = 56 * 1024 * 1024


def _cparams(*sem):
    return pltpu.CompilerParams(dimension_semantics=sem, vmem_limit_bytes=V7X_VMEM_LIMIT_BYTES)


def _tile(n, pref):
    t = min(pref, n)
    while n % t:
        t //= 2
    return t


def _dot(a, b):
    return jnp.dot(a, b, preferred_element_type=F32)


def _dot_nt(a, b):
    return lax.dot_general(a, b, (((1,), (1,)), ((), ())), preferred_element_type=F32)


def _silu(x):
    return x * jax.nn.sigmoid(x)


def _div_pow2(x, n):
    assert n & (n - 1) == 0
    return jnp.right_shift(x, n.bit_length() - 1)


def _mod_kernel(c_ref, w_ref, b_ref, o_ref):
    cs = _silu(c_ref[...]).astype(BF16)
    o_ref[0] = _dot(cs, w_ref[0].astype(BF16)) + b_ref[0]


def _modulation(c_pad, w_ada, b_ada):
    depth, d, n9 = w_ada.shape
    tn = _tile(n9, 1024)
    return pl.pallas_call(
        _mod_kernel,
        grid=(depth, n9 // tn),
        in_specs=[
            pl.BlockSpec((8, d), lambda l, n: (0, 0)),
            pl.BlockSpec((1, d, tn), lambda l, n: (l, 0, n)),
            pl.BlockSpec((1, 1, tn), lambda l, n: (l, 0, n)),
        ],
        out_specs=pl.BlockSpec((1, 8, tn), lambda l, n: (l, 0, n)),
        out_shape=jax.ShapeDtypeStruct((depth, 8, n9), F32),
        compiler_params=_cparams("arbitrary", "arbitrary"),
        name="adaln_mod",
    )(c_pad, w_ada, b_ada.reshape(depth, 1, n9))


def _norm_kernel(x_ref, g_ref, mod_ref, o_ref, *, shift_idx, scale_idx):
    x = x_ref[...]
    y = x * lax.rsqrt(jnp.mean(x * x, axis=-1, keepdims=True) + RMS_EPS) * g_ref[...]
    if scale_idx is not None:
        y = y * (1.0 + mod_ref[0, scale_idx:scale_idx + 1, :]) + mod_ref[0, shift_idx:shift_idx + 1, :]
    o_ref[...] = y.astype(o_ref.dtype)


def _norm(x, g, mod_l, seq, shift_idx, scale_idx, out_dtype):
    t, d = x.shape
    tr = _tile(seq, 512)
    per_b = seq // tr
    return pl.pallas_call(
        functools.partial(_norm_kernel, shift_idx=shift_idx, scale_idx=scale_idx),
        grid=(t // tr,),
        in_specs=[
            pl.BlockSpec((tr, d), lambda i: (i, 0)),
            pl.BlockSpec((1, d), lambda i: (0, 0)),
            pl.BlockSpec((1, N_MOD, d), lambda i: (i // per_b, 0, 0)),
        ],
        out_specs=pl.BlockSpec((tr, d), lambda i: (i, 0)),
        out_shape=jax.ShapeDtypeStruct((t, d), out_dtype),
        compiler_params=_cparams("arbitrary"),
        name="rms_adaln",
    )(x, g.reshape(1, d), mod_l)


def _gu_kernel(h_ref, wg_ref, wu_ref, o_ref, *, rc):
    wg = wg_ref[...].astype(BF16)
    wu = wu_ref[...].astype(BF16)
    for r in range(h_ref.shape[0] // rc):
        rows = slice(r * rc, (r + 1) * rc)
        h = h_ref[rows, :]
        g = _dot(h, wg)
        u = _dot(h, wu)
        o_ref[rows, :] = (_silu(g) * u).astype(BF16)


def _ffn_gate_up(h, w_gu, l, k):
    t, d = h.shape
    dff = w_gu.shape[-1] // 2
    tm = _tile(t, 2048)
    tf = _tile(dff, 256)
    nf = dff // tf
    return pl.pallas_call(
        functools.partial(_gu_kernel, rc=_tile(tm, 512)),
        grid=(t // tm, nf),
        in_specs=[
            pl.BlockSpec((tm, d), lambda i, f: (i, 0)),
            pl.BlockSpec((None, None, d, tf), lambda i, f: (l, k, 0, f)),
            pl.BlockSpec((None, None, d, tf), lambda i, f: (l, k, 0, f + nf)),
        ],
        out_specs=pl.BlockSpec((tm, tf), lambda i, f: (i, f)),
        out_shape=jax.ShapeDtypeStruct((t, dff), BF16),
        compiler_params=_cparams("arbitrary", "arbitrary"),
        name="ffn_gate_up",
    )(h, w_gu, w_gu)


def _down_kernel(a_ref, w_ref, x_ref, mod_ref, o_ref, *, gate_idx, coef, rc):
    k = pl.program_id(2)
    w = w_ref[...].astype(BF16)

    @pl.when(k == 0)
    def _():
        o_ref[...] = jnp.zeros_like(o_ref)

    for r in range(a_ref.shape[0] // rc):
        rows = slice(r * rc, (r + 1) * rc)
        o_ref[rows, :] += _dot(a_ref[rows, :], w)

    @pl.when(k == pl.num_programs(2) - 1)
    def _():
        gate = coef * mod_ref[0, gate_idx:gate_idx + 1, :]
        o_ref[...] = x_ref[...] + gate * o_ref[...]


def _gated_residual_matmul(a, w_full, w_prefix, x, mod_l, seq, gate_idx, coef):
    t, kdim = a.shape
    d = x.shape[1]
    tm = _tile(seq, 2048)
    tn = _tile(d, 1024)
    tk = _tile(kdim, 512)
    per_b = seq // tm
    npre = len(w_prefix)
    return pl.pallas_call(
        functools.partial(_down_kernel, gate_idx=gate_idx, coef=coef, rc=_tile(tm, 512)),
        grid=(t // tm, d // tn, kdim // tk),
        in_specs=[
            pl.BlockSpec((tm, tk), lambda i, n, k: (i, k)),
            pl.BlockSpec((None,) * npre + (tk, tn), lambda i, n, k: w_prefix + (k, n)),
            pl.BlockSpec((tm, tn), lambda i, n, k: (i, n)),
            pl.BlockSpec((1, N_MOD, tn), lambda i, n, k: (i // per_b, 0, n)),
        ],
        out_specs=pl.BlockSpec((tm, tn), lambda i, n, k: (i, n)),
        out_shape=jax.ShapeDtypeStruct((t, d), F32),
        compiler_params=_cparams("arbitrary", "arbitrary", "arbitrary"),
        name="gated_residual_matmul",
    )(a, w_full, x, mod_l)


def _rope_tile(x, cos, sin_signed, half, period):
    lane = lax.broadcasted_iota(jnp.int32, x.shape, 1)
    first = (lane & (period - 1)) < half
    partner = jnp.where(first, pltpu.roll(x, V7X_LANES - half, 1), pltpu.roll(x, half, 1))
    return x * cos + partner * sin_signed


def _in_kernel(h_ref, w_ref, wgate_ref, cd_ref, sd_ref, cn_ref, sn_ref,
               o_ref, okv_ref, og_ref, acc_ref, *, cols, rc):
    n = pl.program_id(1)
    tm = h_ref.shape[0]
    w = w_ref[...].astype(BF16)
    chunks = [slice(r * rc, (r + 1) * rc) for r in range(tm // rc)]
    for rows in chunks:
        acc_ref[rows, :] = _dot(h_ref[rows, :], w)

    @pl.when(n == 0)
    def _():
        wg = wgate_ref[...].astype(BF16)
        for rows in chunks:
            og_ref[rows, :] = _dot(h_ref[rows, :], wg)

    is_dq = n < cols["dk"]
    is_diff = n < cols["dv"]
    is_nq = (n >= cols["nq"]) & (n < cols["kc"])
    is_nsa = is_nq | (n == cols["ks"]) | (n == cols["kw"])
    is_kvc = (n >= cols["kc"]) & (n < cols["ks"])

    def rope_store(cos_ref, sin_ref, half, period, scale):
        for rows in chunks:
            x = acc_ref[rows, :]
            c = cos_ref[rows, :]
            s = sin_ref[rows, :]
            halves = [_rope_tile(x[:, j * V7X_LANES:(j + 1) * V7X_LANES], c, s, half, period)
                      for j in range(x.shape[1] // V7X_LANES)]
            o_ref[rows, :] = (jnp.concatenate(halves, axis=1) * scale).astype(BF16)

    @pl.when(is_diff)
    def _():
        rope_store(cd_ref, sd_ref, DIFF_QK_DIM // ROPE_FRACTION // 2, DIFF_QK_DIM,
                   jnp.where(is_dq, DIFF_QK_DIM ** -0.5, 1.0))

    @pl.when(is_nsa)
    def _():
        rope_store(cn_ref, sn_ref, HEAD_DIM // ROPE_FRACTION // 2, HEAD_DIM,
                   jnp.where(is_nq, HEAD_DIM ** -0.5, 1.0))

    @pl.when(jnp.logical_not(is_diff | is_nsa))
    def _():
        o_ref[...] = acc_ref[...].astype(BF16)

    @pl.when(is_kvc)
    def _():
        okv_ref[...] = acc_ref[...]


def _input_projection(h, w_in, l, w_gate, tabs, seq, cols, n_main):
    t, d = h.shape
    tm = _tile(seq, 2048)
    tn = 2 * V7X_LANES
    per_b = seq // tm
    nb = n_main // tn
    kc0 = cols["kc"]
    tab_spec = pl.BlockSpec((tm, V7X_LANES), lambda i, n: (i % per_b, 0))
    return pl.pallas_call(
        functools.partial(_in_kernel, cols=cols, rc=_tile(tm, 512)),
        grid=(t // tm, nb),
        in_specs=[
            pl.BlockSpec((tm, d), lambda i, n: (i, 0)),
            pl.BlockSpec((None, d, tn), lambda i, n: (l, 0, n)),
            pl.BlockSpec((d, V7X_LANES), lambda i, n: (0, 0)),
            tab_spec, tab_spec, tab_spec, tab_spec,
        ],
        out_specs=[
            pl.BlockSpec((tm, tn), lambda i, n: (i, n)),
            pl.BlockSpec((tm, tn), lambda i, n: (i, jnp.clip(n - kc0, 0, 1))),
            pl.BlockSpec((tm, V7X_LANES), lambda i, n: (i, 0)),
        ],
        out_shape=[
            jax.ShapeDtypeStruct((t, n_main), BF16),
            jax.ShapeDtypeStruct((t, 2 * tn), F32),
            jax.ShapeDtypeStruct((t, V7X_LANES), F32),
        ],
        scratch_shapes=[pltpu.VMEM((tm, tn), F32)],
        compiler_params=_cparams("arbitrary", "arbitrary"),
        name="input_projection",
    )(h, w_in, w_gate, *tabs)


def _rope_tables(seq, rot_dim, period):
    half = rot_dim // 2
    inv = jnp.exp(-math.log(ROPE_THETA) * jnp.arange(half, dtype=F32) / half)
    ang = jnp.arange(seq, dtype=F32)[:, None] * inv[None, :]
    cos, sin = jnp.cos(ang), jnp.sin(ang)
    pad = jnp.zeros((seq, period - rot_dim), F32)
    cos_p = jnp.concatenate([cos, cos, pad + 1.0], axis=1)
    sin_p = jnp.concatenate([-sin, sin, pad], axis=1)
    reps = V7X_LANES // period
    return jnp.tile(cos_p, (1, reps)), jnp.tile(sin_p, (1, reps))


def _diff_kernel(lam_ref, sg_ref, q_ref, k_ref, v_ref, o_ref, m_ref, l_ref, acc_ref, *, lam_init, tk):
    i = pl.program_id(2)
    tq = q_ref.shape[0]
    q = q_ref[...]
    lane = lax.broadcasted_iota(jnp.int32, q.shape, 1)
    zero = jnp.zeros_like(q)
    q2 = jnp.concatenate([jnp.where(lane < DIFF_QK_DIM, q, zero),
                          jnp.where(lane >= DIFF_QK_DIM, q, zero)], axis=0)
    m_ref[...] = jnp.full_like(m_ref, NEG)
    l_ref[...] = jnp.zeros_like(l_ref)
    acc_ref[...] = jnp.zeros_like(acc_ref)

    def step(j, masked):
        start = pl.multiple_of(j * tk, tk)
        kt = k_ref[pl.ds(start, tk), :]
        vt = v_ref[pl.ds(start, tk), :]
        s = _dot_nt(q2, kt)
        if masked:
            row = lax.broadcasted_iota(jnp.int32, s.shape, 0)
            row = jnp.where(row >= tq, row - tq, row)
            col = lax.broadcasted_iota(jnp.int32, s.shape, 1)
            s = jnp.where(col <= row, s, NEG)
        m_old = m_ref[...]
        m_new = jnp.maximum(m_old, jnp.max(s, axis=-1, keepdims=True))
        alpha = jnp.exp(m_old - m_new)
        p = jnp.exp(s - m_new)
        l_ref[...] = alpha * l_ref[...] + jnp.sum(p, axis=-1, keepdims=True)
        acc_ref[...] = alpha * acc_ref[...] + _dot(p.astype(BF16), vt)
        m_ref[...] = m_new

    def body(j, c):
        step(j, False)
        return c

    lax.fori_loop(0, i, body, 0)
    step(i, True)

    lam_p = lam_ref[...]
    lam = (jnp.exp(jnp.sum(lam_p[0:1] * lam_p[1:2], axis=-1, keepdims=True))
           - jnp.exp(jnp.sum(lam_p[2:3] * lam_p[3:4], axis=-1, keepdims=True)) + lam_init)
    o_all = acc_ref[...] / l_ref[...]
    o = o_all[:tq] - lam * o_all[tq:]
    o = o * lax.rsqrt(jnp.mean(o * o, axis=-1, keepdims=True) + RMS_EPS) * sg_ref[...]
    o_ref[...] = (o * (1.0 - lam_init)).astype(o_ref.dtype)


def _diff_attention(proj, lam_p, subln_g, batch, seq, n_heads, cols, lam_init):
    tq = _tile(seq, 256)
    nq = seq // tq
    k0 = cols["dk"] * 2
    v0 = cols["dv"] * 2
    kv_spec = lambda c0: pl.BlockSpec((seq, HEAD_DIM), lambda b, h, i: (b, c0 + h))
    return pl.pallas_call(
        functools.partial(_diff_kernel, lam_init=lam_init, tk=tq),
        grid=(batch, n_heads, nq),
        in_specs=[
            pl.BlockSpec((4, DIFF_QK_DIM), lambda b, h, i: (0, 0)),
            pl.BlockSpec((1, HEAD_DIM), lambda b, h, i: (0, 0)),
            pl.BlockSpec((tq, HEAD_DIM), lambda b, h, i: (b * nq + i, h)),
            kv_spec(k0),
            kv_spec(v0),
        ],
        out_specs=pl.BlockSpec((tq, HEAD_DIM), lambda b, h, i: (b * nq + i, h)),
        out_shape=jax.ShapeDtypeStruct((batch * seq, n_heads * HEAD_DIM), BF16),
        scratch_shapes=[pltpu.VMEM((2 * tq, 1), F32), pltpu.VMEM((2 * tq, 1), F32),
                        pltpu.VMEM((2 * tq, HEAD_DIM), F32)],
        compiler_params=_cparams("arbitrary", "arbitrary", "arbitrary"),
        name="diff_attention",
    )(lam_p, subln_g.reshape(1, HEAD_DIM), proj, proj, proj)


def _cmp_kernel(x_ref, pe_ref, w1_ref, w2_ref, o_ref):
    x = x_ref[0, 0, 0]
    half = x.shape[1]
    xa = (x + pe_ref[0, 0:1, :]).astype(BF16)
    xb = (x + pe_ref[0, 1:2, :]).astype(BF16)
    ya = _dot(xa, w1_ref[0, :half, :].astype(BF16))
    yb = _dot(xb, w1_ref[0, half:, :].astype(BF16))
    nrow = x.shape[0]
    row = lax.broadcasted_iota(jnp.int32, yb.shape, 0)
    yb_next = jnp.where(row < nrow - 1, pltpu.roll(yb, nrow - 1, 0), 0.0)
    hid = _silu(ya + yb_next).astype(BF16)
    o_ref[0, 0, 0] = _dot(hid, w2_ref[0].astype(BF16))


def _compress(xt, pe2, w1, w2):
    _, batch, groups, nrow, wide = xt.shape
    hidden = w1.shape[-1]
    return pl.pallas_call(
        _cmp_kernel,
        grid=(2, batch, groups),
        in_specs=[
            pl.BlockSpec((1, 1, 1, nrow, wide), lambda j, b, g: (j, b, g, 0, 0)),
            pl.BlockSpec((1, 2, wide), lambda j, b, g: (j, 0, 0)),
            pl.BlockSpec((1, 2 * wide, hidden), lambda j, b, g: (j, 0, 0)),
            pl.BlockSpec((1, hidden, HEAD_DIM), lambda j, b, g: (j, 0, 0)),
        ],
        out_specs=pl.BlockSpec((1, 1, 1, nrow, HEAD_DIM), lambda j, b, g: (j, b, g, 0, 0)),
        out_shape=jax.ShapeDtypeStruct((2, batch, groups, nrow, HEAD_DIM), F32),
        compiler_params=_cparams("arbitrary", "arbitrary", "arbitrary"),
        name="nsa_compress",
    )(xt, pe2, w1, w2)


def _cmpattn_kernel(q_ref, kc_ref, vc_ref, oc_ref, sel_ref, *, n_cmp, n_sel, hpg):
    i = pl.program_id(2)
    tq = q_ref.shape[0]
    kc = kc_ref[0, 0, 0].astype(BF16)
    vc = vc_ref[0, 0, 0].astype(BF16)
    ncp = kc.shape[0]
    tpos = i * tq + lax.broadcasted_iota(jnp.int32, (tq, ncp), 0)
    nidx = lax.broadcasted_iota(jnp.int32, (tq, ncp), 1)
    cmask = (nidx * NSA_CMP_STRIDE + NSA_CMP_LEN - 1 <= tpos) & (nidx < n_cmp)
    psum = jnp.zeros((tq, ncp), F32)
    for j in range(hpg):
        qj = q_ref[:, j * HEAD_DIM:(j + 1) * HEAD_DIM]
        s = jnp.where(cmask, _dot_nt(qj, kc), NEG)
        m = jnp.max(s, axis=-1, keepdims=True)
        e = jnp.where(cmask, jnp.exp(s - m), 0.0)
        den = jnp.sum(e, axis=-1, keepdims=True)
        p = e / jnp.where(den > 0.0, den, 1.0)
        oc_ref[:, j * HEAD_DIM:(j + 1) * HEAD_DIM] = _dot(p.astype(BF16), vc)
        psum = psum + p

    cn = lax.broadcasted_iota(jnp.int32, (ncp, n_sel), 0) * NSA_CMP_STRIDE
    sm = lax.broadcasted_iota(jnp.int32, (ncp, n_sel), 1) * NSA_SEL_BLOCK
    ov = jnp.maximum(jnp.minimum(cn + NSA_CMP_LEN, sm + NSA_SEL_BLOCK) - jnp.maximum(cn, sm), 0)
    ov = (ov.astype(F32) / NSA_CMP_LEN).astype(BF16)
    p_hi = psum.astype(BF16)
    p_lo = (psum - p_hi.astype(F32)).astype(BF16)
    imp = _dot(p_hi, ov) + _dot(p_lo, ov)

    t = i * tq + lax.broadcasted_iota(jnp.int32, (tq, n_sel), 0)
    blk = lax.broadcasted_iota(jnp.int32, (tq, n_sel), 1)
    cur = _div_pow2(t, NSA_SEL_BLOCK)
    valid = blk * NSA_SEL_BLOCK <= t
    forced = (blk == 0) | (blk == cur) | (blk == cur - 1)
    score = jnp.where(valid, imp + jnp.where(forced, FORCE_BONUS, 0.0), NEG)
    rank = jnp.zeros((tq, n_sel), F32)
    for c in range(n_sel):
        col = jnp.broadcast_to(score[:, c:c + 1], (tq, n_sel))
        ahead = (col > score) | ((col == score) & (blk > c))
        rank = rank + jnp.where(ahead, 1.0, 0.0)
    n_top = min(NSA_TOP_N, n_sel)
    sel_ref[0, 0] = jnp.where((rank < n_top) & valid, 1.0, 0.0).astype(sel_ref.dtype)


def _compressed_attention(proj, kvc, batch, seq, groups, hpg, cols):
    tq = _tile(seq, 256)
    nq = seq // tq
    ncp = seq // NSA_CMP_STRIDE
    n_cmp = (seq - NSA_CMP_LEN) // NSA_CMP_STRIDE + 1
    n_sel = seq // NSA_SEL_BLOCK
    wq = hpg * HEAD_DIM
    q0 = cols["nq"] * 2 * V7X_LANES // wq
    kv_spec = lambda j: pl.BlockSpec((1, 1, 1, ncp, HEAD_DIM), lambda b, g, i: (j, b, g, 0, 0))
    return pl.pallas_call(
        functools.partial(_cmpattn_kernel, n_cmp=n_cmp, n_sel=n_sel, hpg=hpg),
        grid=(batch, groups, nq),
        in_specs=[
            pl.BlockSpec((tq, wq), lambda b, g, i: (b * nq + i, q0 + g)),
            kv_spec(0),
            kv_spec(1),
        ],
        out_specs=[
            pl.BlockSpec((tq, wq), lambda b, g, i: (b * nq + i, g)),
            pl.BlockSpec((1, 1, tq, n_sel), lambda b, g, i: (b, g, i, 0)),
        ],
        out_shape=[
            jax.ShapeDtypeStruct((batch * seq, groups * wq), F32),
            jax.ShapeDtypeStruct((batch, groups, seq, n_sel), BF16),
        ],
        compiler_params=_cparams("arbitrary", "arbitrary", "arbitrary"),
        name="nsa_compressed_attention",
    )(proj, kvc, kvc)


def _sel_kernel(q_ref, ks_ref, vs_ref, kw_ref, vw_ref, sel_ref, oc_ref, gl_ref, o_ref,
                m_ref, l_ref, acc_ref, *, hpg, tk, g_axis):
    i = pl.program_id(2)
    g = pl.program_id(g_axis)
    tq = q_ref.shape[0]
    q0 = i * tq
    q4 = jnp.concatenate([q_ref[:, j * HEAD_DIM:(j + 1) * HEAD_DIM] for j in range(hpg)], axis=0)
    sel = sel_ref[0, 0]
    n_sel = sel.shape[1]

    m_ref[...] = jnp.full_like(m_ref, NEG)
    l_ref[...] = jnp.zeros_like(l_ref)
    acc_ref[...] = jnp.zeros_like(acc_ref)

    def body(j, c):
        start = pl.multiple_of(j * tk, tk)
        kt = ks_ref[pl.ds(start, tk), :]
        vt = vs_ref[pl.ds(start, tk), :]
        kpos = start + lax.broadcasted_iota(jnp.int32, (tk, n_sel), 0)
        blk = lax.broadcasted_iota(jnp.int32, (tk, n_sel), 1)
        expand = jnp.where(_div_pow2(kpos, NSA_SEL_BLOCK) == blk, 1.0, 0.0).astype(BF16)
        chosen = _dot_nt(sel, expand)
        qpos = q0 + lax.broadcasted_iota(jnp.int32, (tq, tk), 0)
        kcol = start + lax.broadcasted_iota(jnp.int32, (tq, tk), 1)
        bias = jnp.where((chosen > 0.5) & (kcol <= qpos), 0.0, NEG)
        s = _dot_nt(q4, kt) + jnp.concatenate([bias] * hpg, axis=0)
        m_old = m_ref[...]
        m_new = jnp.maximum(m_old, jnp.max(s, axis=-1, keepdims=True))
        alpha = jnp.exp(m_old - m_new)
        p = jnp.exp(s - m_new)
        l_ref[...] = alpha * l_ref[...] + jnp.sum(p, axis=-1, keepdims=True)
        acc_ref[...] = alpha * acc_ref[...] + _dot(p.astype(BF16), vt)
        m_ref[...] = m_new
        return c

    lax.fori_loop(0, (q0 + tq + tk - 1) // tk, body, 0)
    o_s = acc_ref[...] / l_ref[...]

    span = NSA_WINDOW + tq
    wstart = pl.multiple_of(jnp.maximum(q0 - NSA_WINDOW, 0), tq)
    kt = kw_ref[pl.ds(wstart, span), :]
    vt = vw_ref[pl.ds(wstart, span), :]
    qpos = q0 + lax.broadcasted_iota(jnp.int32, (tq, span), 0)
    kcol = wstart + lax.broadcasted_iota(jnp.int32, (tq, span), 1)
    dist = qpos - kcol
    wbias = jnp.where((dist >= 0) & (dist < NSA_WINDOW), 0.0, NEG)
    s = _dot_nt(q4, kt) + jnp.concatenate([wbias] * hpg, axis=0)
    m = jnp.max(s, axis=-1, keepdims=True)
    p = jnp.exp(s - m)
    o_w = _dot(p.astype(BF16), vt) / jnp.sum(p, axis=-1, keepdims=True)

    gates = jax.nn.sigmoid(gl_ref[...])
    lane = lax.broadcasted_iota(jnp.int32, gates.shape, 1)
    for j in range(hpg):
        head = g * hpg + j
        rows = slice(j * tq, (j + 1) * tq)
        gsel = [jnp.sum(jnp.where(lane == 3 * head + c, gates, 0.0), axis=-1, keepdims=True) for c in range(3)]
        o = (gsel[0] * oc_ref[:, j * HEAD_DIM:(j + 1) * HEAD_DIM] + gsel[1] * o_s[rows] + gsel[2] * o_w[rows])
        o_ref[:, j * HEAD_DIM:(j + 1) * HEAD_DIM] = o.astype(o_ref.dtype)


def _selected_window_attention(proj, sel, o_c, gl, batch, seq, groups, hpg, cols):
    tq = _tile(seq, 128)
    nq = seq // tq
    n_sel = seq // NSA_SEL_BLOCK
    wq = hpg * HEAD_DIM
    q0 = cols["nq"] * 2 * V7X_LANES // wq
    kv_spec = lambda name, off: pl.BlockSpec(
        (seq, HEAD_DIM), lambda b, g, i: (b, (cols[name] * 2 + off * groups) + g))
    return pl.pallas_call(
        functools.partial(_sel_kernel, hpg=hpg, tk=_tile(seq, 512), g_axis=1),
        grid=(batch, groups, nq),
        in_specs=[
            pl.BlockSpec((tq, wq), lambda b, g, i: (b * nq + i, q0 + g)),
            kv_spec("ks", 0), kv_spec("ks", 1), kv_spec("kw", 0), kv_spec("kw", 1),
            pl.BlockSpec((1, 1, tq, n_sel), lambda b, g, i: (b, g, i, 0)),
            pl.BlockSpec((tq, wq), lambda b, g, i: (b * nq + i, g)),
            pl.BlockSpec((tq, V7X_LANES), lambda b, g, i: (b * nq + i, 0)),
        ],
        out_specs=pl.BlockSpec((tq, wq), lambda b, g, i: (b * nq + i, g)),
        out_shape=jax.ShapeDtypeStruct((batch * seq, groups * wq), BF16),
        scratch_shapes=[pltpu.VMEM((hpg * tq, 1), F32), pltpu.VMEM((hpg * tq, 1), F32),
                        pltpu.VMEM((hpg * tq, HEAD_DIM), F32)],
        compiler_params=_cparams("arbitrary", "arbitrary", "arbitrary"),
        name="nsa_selected_window",
    )(proj, proj, proj, proj, proj, sel, o_c, gl)


def _out_kernel(od_ref, on_ref, w_ref, x_ref, mod_ref, o_ref, *, gate_idx, rc):
    kd = od_ref.shape[1]
    w = w_ref[...].astype(BF16)
    gate = mod_ref[0, gate_idx:gate_idx + 1, :]
    for r in range(od_ref.shape[0] // rc):
        rows = slice(r * rc, (r + 1) * rc)
        y = _dot(od_ref[rows, :], w[:kd]) + _dot(on_ref[rows, :], w[kd:])
        o_ref[rows, :] = x_ref[rows, :] + gate * y


def _output_projection(o_diff, o_nsa, w_o, l, x, mod_l, seq, gate_idx):
    t, d = x.shape
    kd, kn = o_diff.shape[1], o_nsa.shape[1]
    tm = _tile(seq, 1024)
    tn = _tile(d, 512)
    per_b = seq // tm
    return pl.pallas_call(
        functools.partial(_out_kernel, gate_idx=gate_idx, rc=_tile(tm, 512)),
        grid=(t // tm, d // tn),
        in_specs=[
            pl.BlockSpec((tm, kd), lambda i, n: (i, 0)),
            pl.BlockSpec((tm, kn), lambda i, n: (i, 0)),
            pl.BlockSpec((None, kd + kn, tn), lambda i, n: (l, 0, n)),
            pl.BlockSpec((tm, tn), lambda i, n: (i, n)),
            pl.BlockSpec((1, N_MOD, tn), lambda i, n: (i // per_b, 0, n)),
        ],
        out_specs=pl.BlockSpec((tm, tn), lambda i, n: (i, n)),
        out_shape=jax.ShapeDtypeStruct((t, d), F32),
        compiler_params=_cparams("arbitrary", "arbitrary"),
        name="output_projection",
    )(o_diff, o_nsa, w_o, x, mod_l)


def kernel(x, c, w_ada, b_ada, norm_g, ffn_w_gu, ffn_w_d, w_in, w_o, diff_lam, diff_subln,
           cmp_pe, cmp_w1, cmp_w2, final_g):
    batch, seq, d = x.shape
    depth = w_ada.shape[0]
    n_diff = d // (2 * HEAD_DIM)
    n_nsa = d // (2 * HEAD_DIM)
    groups = N_NSA_KV
    hpg = n_nsa // groups
    diff_w = n_diff * HEAD_DIM
    nsa_w = n_nsa * HEAD_DIM
    kv_w = groups * HEAD_DIM
    n_main = 3 * diff_w + nsa_w + 6 * kv_w
    assert w_in.shape[-1] == n_main + 3 * n_nsa
    cw = 2 * V7X_LANES
    assert diff_w % cw == 0 and nsa_w % cw == 0 and kv_w == cw
    assert seq >= NSA_WINDOW + 128
    cols = {"dk": diff_w // cw, "dv": 2 * diff_w // cw, "nq": 3 * diff_w // cw}
    cols["kc"] = cols["nq"] + nsa_w // cw
    cols.update(vc=cols["kc"] + 1, ks=cols["kc"] + 2, vs=cols["kc"] + 3, kw=cols["kc"] + 4, vw=cols["kc"] + 5)

    t = batch * seq
    xf = x.reshape(t, d)
    c_pad = jnp.pad(c, ((0, 8 - batch), (0, 0)))
    mod = _modulation(c_pad, w_ada, b_ada).reshape(depth, 8, N_MOD, d)
    tabs = (_rope_tables(seq, DIFF_QK_DIM // ROPE_FRACTION, DIFF_QK_DIM)
            + _rope_tables(seq, HEAD_DIM // ROPE_FRACTION, HEAD_DIM))
    ncp = seq // NSA_CMP_STRIDE
    cmp_half = NSA_CMP_STRIDE * HEAD_DIM

    for l in range(depth):
        mod_l = mod[l]
        lam_init = 0.8 - 0.6 * math.exp(-0.3 * l)

        h = _norm(xf, norm_g[l, 0], mod_l, seq, 0, 1, BF16)
        a = _ffn_gate_up(h, ffn_w_gu, l, 0)
        xf = _gated_residual_matmul(a, ffn_w_d, (l, 0), xf, mod_l, seq, 2, 0.5)

        h = _norm(xf, norm_g[l, 1], mod_l, seq, 3, 4, BF16)
        w_gate = jnp.pad(w_in[l, :, n_main:], ((0, 0), (0, V7X_LANES - 3 * n_nsa)))
        proj, kvc_raw, gl = _input_projection(h, w_in, l, w_gate, tabs, seq, cols, n_main)
        o_diff = _diff_attention(proj, diff_lam[l], diff_subln[l], batch, seq, n_diff, cols, lam_init)
        xt = kvc_raw.reshape(batch, ncp, NSA_CMP_STRIDE, 2, groups, HEAD_DIM)
        xt = xt.transpose(3, 0, 4, 1, 2, 5).reshape(2, batch, groups, ncp, cmp_half)
        pe2 = cmp_pe[l].reshape(2, 2, cmp_half)
        kvc = _compress(xt, pe2, cmp_w1[l], cmp_w2[l])
        o_c, sel = _compressed_attention(proj, kvc, batch, seq, groups, hpg, cols)
        o_nsa = _selected_window_attention(proj, sel, o_c, gl, batch, seq, groups, hpg, cols)
        xf = _output_projection(o_diff, o_nsa, w_o, l, xf, mod_l, seq, 5)

        h = _norm(xf, norm_g[l, 2], mod_l, seq, 6, 7, BF16)
        a = _ffn_gate_up(h, ffn_w_gu, l, 1)
        xf = _gated_residual_matmul(a, ffn_w_d, (l, 1), xf, mod_l, seq, 8, 0.5)

    out = _norm(xf, final_g, mod[0], seq, None, None, F32)
    return out.reshape(batch, seq, d)
```

```python
import functools
import math

import jax
import jax.numpy as jnp
from jax import lax
from jax.experimental import pallas as pl
from jax.experimental.pallas import tpu as pltpu

F32 = jnp.float32
BF16 = jnp.bfloat16

HEAD_DIM = 128
DIFF_QK_DIM = HEAD_DIM // 2
N_NSA_KV = 2
NSA_CMP_LEN = 32
NSA_CMP_STRIDE = 16
NSA_SEL_BLOCK = 64
NSA_TOP_N = 16
NSA_WINDOW = 512
ROPE_THETA = 500000.0
ROPE_FRACTION = 4
N_MOD = 9
RMS_EPS = 1e-6
NEG = -1e30
FORCE_BONUS = 1e6

V7X_LANES = 128
V7X_VMEM_LIMIT_BYTES = 56 * 1024 * 1024


def _cparams(*sem):
    return pltpu.CompilerParams(dimension_semantics=sem, vmem_limit_bytes=V7X_VMEM_LIMIT_BYTES)


def _tile(n, pref):
    t = min(pref, n)
    while n % t:
        t //= 2
    return t


def _dot(a, b):
    return jnp.dot(a, b, preferred_element_type=F32)


def _dot_nt(a, b):
    return lax.dot_general(a, b, (((1,), (1,)), ((), ())), preferred_element_type=F32)


def _silu(x):
    return x * jax.nn.sigmoid(x)


def _div_pow2(x, n):
    assert n & (n - 1) == 0
    return jnp.right_shift(x, n.bit_length() - 1)


def _mod_kernel(c_ref, w_ref, b_ref, o_ref):
    cs = _silu(c_ref[...]).astype(BF16)
    o_ref[0] = _dot(cs, w_ref[0].astype(BF16)) + b_ref[0]


def _modulation(c_pad, w_ada, b_ada):
    depth, d, n9 = w_ada.shape
    tn = _tile(n9, 1024)
    return pl.pallas_call(
        _mod_kernel,
        grid=(depth, n9 // tn),
        in_specs=[
            pl.BlockSpec((8, d), lambda l, n: (0, 0)),
            pl.BlockSpec((1, d, tn), lambda l, n: (l, 0, n)),
            pl.BlockSpec((1, 1, tn), lambda l, n: (l, 0, n)),
        ],
        out_specs=pl.BlockSpec((1, 8, tn), lambda l, n: (l, 0, n)),
        out_shape=jax.ShapeDtypeStruct((depth, 8, n9), F32),
        compiler_params=_cparams("arbitrary", "arbitrary"),
        name="adaln_mod",
    )(c_pad, w_ada, b_ada.reshape(depth, 1, n9))


def _norm_kernel(x_ref, g_ref, mod_ref, o_ref, *, shift_idx, scale_idx):
    x = x_ref[...]
    y = x * lax.rsqrt(jnp.mean(x * x, axis=-1, keepdims=True) + RMS_EPS) * g_ref[...]
    if scale_idx is not None:
        y = y * (1.0 + mod_ref[0, scale_idx:scale_idx + 1, :]) + mod_ref[0, shift_idx:shift_idx + 1, :]
    o_ref[...] = y.astype(o_ref.dtype)


def _norm(x, g, mod_l, seq, shift_idx, scale_idx, out_dtype):
    t, d = x.shape
    tr = _tile(seq, 512)
    per_b = seq // tr
    return pl.pallas_call(
        functools.partial(_norm_kernel, shift_idx=shift_idx, scale_idx=scale_idx),
        grid=(t // tr,),
        in_specs=[
            pl.BlockSpec((tr, d), lambda i: (i, 0)),
            pl.BlockSpec((1, d), lambda i: (0, 0)),
            pl.BlockSpec((1, N_MOD, d), lambda i: (i // per_b, 0, 0)),
        ],
        out_specs=pl.BlockSpec((tr, d), lambda i: (i, 0)),
        out_shape=jax.ShapeDtypeStruct((t, d), out_dtype),
        compiler_params=_cparams("arbitrary"),
        name="rms_adaln",
    )(x, g.reshape(1, d), mod_l)


def _gu_kernel(h_ref, wg_ref, wu_ref, o_ref, *, rc):
    wg = wg_ref[...].astype(BF16)
    wu = wu_ref[...].astype(BF16)
    for r in range(h_ref.shape[0] // rc):
        rows = slice(r * rc, (r + 1) * rc)
        h = h_ref[rows, :]
        g = _dot(h, wg)
        u = _dot(h, wu)
        o_ref[rows, :] = (_silu(g) * u).astype(BF16)


def _ffn_gate_up(h, w_gu, l, k):
    t, d = h.shape
    dff = w_gu.shape[-1] // 2
    tm = _tile(t, 2048)
    tf = _tile(dff, 256)
    nf = dff // tf
    return pl.pallas_call(
        functools.partial(_gu_kernel, rc=_tile(tm, 512)),
        grid=(t // tm, nf),
        in_specs=[
            pl.BlockSpec((tm, d), lambda i, f: (i, 0)),
            pl.BlockSpec((None, None, d, tf), lambda i, f: (l, k, 0, f)),
            pl.BlockSpec((None, None, d, tf), lambda i, f: (l, k, 0, f + nf)),
        ],
        out_specs=pl.BlockSpec((tm, tf), lambda i, f: (i, f)),
        out_shape=jax.ShapeDtypeStruct((t, dff), BF16),
        compiler_params=_cparams("arbitrary", "arbitrary"),
        name="ffn_gate_up",
    )(h, w_gu, w_gu)


def _down_kernel(a_ref, w_ref, x_ref, mod_ref, o_ref, *, gate_idx, coef, rc):
    k = pl.program_id(2)
    w = w_ref[...].astype(BF16)

    @pl.when(k == 0)
    def _():
        o_ref[...] = jnp.zeros_like(o_ref)

    for r in range(a_ref.shape[0] // rc):
        rows = slice(r * rc, (r + 1) * rc)
        o_ref[rows, :] += _dot(a_ref[rows, :], w)

    @pl.when(k == pl.num_programs(2) - 1)
    def _():
        gate = coef * mod_ref[0, gate_idx:gate_idx + 1, :]
        o_ref[...] = x_ref[...] + gate * o_ref[...]


def _gated_residual_matmul(a, w_full, w_prefix, x, mod_l, seq, gate_idx, coef):
    t, kdim = a.shape
    d = x.shape[1]
    tm = _tile(seq, 2048)
    tn = _tile(d, 1024)
    tk = _tile(kdim, 512)
    per_b = seq // tm
    npre = len(w_prefix)
    return pl.pallas_call(
        functools.partial(_down_kernel, gate_idx=gate_idx, coef=coef, rc=_tile(tm, 512)),
        grid=(t // tm, d // tn, kdim // tk),
        in_specs=[
            pl.BlockSpec((tm, tk), lambda i, n, k: (i, k)),
            pl.BlockSpec((None,) * npre + (tk, tn), lambda i, n, k: w_prefix + (k, n)),
            pl.BlockSpec((tm, tn), lambda i, n, k: (i, n)),
            pl.BlockSpec((1, N_MOD, tn), lambda i, n, k: (i // per_b, 0, n)),
        ],
        out_specs=pl.BlockSpec((tm, tn), lambda i, n, k: (i, n)),
        out_shape=jax.ShapeDtypeStruct((t, d), F32),
        compiler_params=_cparams("arbitrary", "arbitrary", "arbitrary"),
        name="gated_residual_matmul",
    )(a, w_full, x, mod_l)


def _rope_tile(x, cos, sin_signed, half, period):
    lane = lax.broadcasted_iota(jnp.int32, x.shape, 1)
    first = (lane & (period - 1)) < half
    partner = jnp.where(first, pltpu.roll(x, V7X_LANES - half, 1), pltpu.roll(x, half, 1))
    return x * cos + partner * sin_signed


def _in_kernel(h_ref, w_ref, wgate_ref, cd_ref, sd_ref, cn_ref, sn_ref,
               o_ref, okv_ref, og_ref, acc_ref, *, cols, rc):
    n = pl.program_id(1)
    tm = h_ref.shape[0]
    w = w_ref[...].astype(BF16)
    chunks = [slice(r * rc, (r + 1) * rc) for r in range(tm // rc)]
    for rows in chunks:
        acc_ref[rows, :] = _dot(h_ref[rows, :], w)

    @pl.when(n == 0)
    def _():
        wg = wgate_ref[...].astype(BF16)
        for rows in chunks:
            og_ref[rows, :] = _dot(h_ref[rows, :], wg)

    is_dq = n < cols["dk"]
    is_diff = n < cols["dv"]
    is_nq = (n >= cols["nq"]) & (n < cols["kc"])
    is_nsa = is_nq | (n == cols["ks"]) | (n == cols["kw"])
    is_kvc = (n >= cols["kc"]) & (n < cols["ks"])

    def rope_store(cos_ref, sin_ref, half, period, scale):
        for rows in chunks:
            x = acc_ref[rows, :]
            c = cos_ref[rows, :]
            s = sin_ref[rows, :]
            halves = [_rope_tile(x[:, j * V7X_LANES:(j + 1) * V7X_LANES], c, s, half, period)
                      for j in range(x.shape[1] // V7X_LANES)]
            o_ref[rows, :] = (jnp.concatenate(halves, axis=1) * scale).astype(BF16)

    @pl.when(is_diff)
    def _():
        rope_store(cd_ref, sd_ref, DIFF_QK_DIM // ROPE_FRACTION // 2, DIFF_QK_DIM,
                   jnp.where(is_dq, DIFF_QK_DIM ** -0.5, 1.0))

    @pl.when(is_nsa)
    def _():
        rope_store(cn_ref, sn_ref, HEAD_DIM // ROPE_FRACTION // 2, HEAD_DIM,
                   jnp.where(is_nq, HEAD_DIM ** -0.5, 1.0))

    @pl.when(jnp.logical_not(is_diff | is_nsa))
    def _():
        o_ref[...] = acc_ref[...].astype(BF16)

    @pl.when(is_kvc)
    def _():
        okv_ref[...] = acc_ref[...]


def _input_projection(h, w_in, l, w_gate, tabs, seq, cols, n_main):
    t, d = h.shape
    tm = _tile(seq, 2048)
    tn = 2 * V7X_LANES
    per_b = seq // tm
    nb = n_main // tn
    kc0 = cols["kc"]
    tab_spec = pl.BlockSpec((tm, V7X_LANES), lambda i, n: (i % per_b, 0))
    return pl.pallas_call(
        functools.partial(_in_kernel, cols=cols, rc=_tile(tm, 512)),
        grid=(t // tm, nb),
        in_specs=[
            pl.BlockSpec((tm, d), lambda i, n: (i, 0)),
            pl.BlockSpec((None, d, tn), lambda i, n: (l, 0, n)),
            pl.BlockSpec((d, V7X_LANES), lambda i, n: (0, 0)),
            tab_spec, tab_spec, tab_spec, tab_spec,
        ],
        out_specs=[
            pl.BlockSpec((tm, tn), lambda i, n: (i, n)),
            pl.BlockSpec((tm, tn), lambda i, n: (i, jnp.clip(n - kc0, 0, 1))),
            pl.BlockSpec((tm, V7X_LANES), lambda i, n: (i, 0)),
        ],
        out_shape=[
            jax.ShapeDtypeStruct((t, n_main), BF16),
            jax.ShapeDtypeStruct((t, 2 * tn), F32),
            jax.ShapeDtypeStruct((t, V7X_LANES), F32),
        ],
        scratch_shapes=[pltpu.VMEM((tm, tn), F32)],
        compiler_params=_cparams("arbitrary", "arbitrary"),
        name="input_projection",
    )(h, w_in, w_gate, *tabs)


def _rope_tables(seq, rot_dim, period):
    half = rot_dim // 2
    inv = jnp.exp(-math.log(ROPE_THETA) * jnp.arange(half, dtype=F32) / half)
    ang = jnp.arange(seq, dtype=F32)[:, None] * inv[None, :]
    cos, sin = jnp.cos(ang), jnp.sin(ang)
    pad = jnp.zeros((seq, period - rot_dim), F32)
    cos_p = jnp.concatenate([cos, cos, pad + 1.0], axis=1)
    sin_p = jnp.concatenate([-sin, sin, pad], axis=1)
    reps = V7X_LANES // period
    return jnp.tile(cos_p, (1, reps)), jnp.tile(sin_p, (1, reps))


def _softmax_update(s, v_tile, m_ref, l_ref, acc_ref):
    m_old = m_ref[...]
    m_new = jnp.maximum(m_old, jnp.max(s, axis=-1, keepdims=True))
    alpha = jnp.exp(m_old - m_new)
    p = jnp.exp(s - m_new)
    if l_ref is not None:
        part = p[:, :V7X_LANES]
        for c in range(1, p.shape[1] // V7X_LANES):
            part = part + p[:, c * V7X_LANES:(c + 1) * V7X_LANES]
        l_ref[...] = alpha * l_ref[...] + part
    acc_ref[...] = alpha * acc_ref[...] + _dot(p.astype(BF16), v_tile)
    m_ref[...] = m_new


def _softmax_finish(l_ref, acc_ref):
    if l_ref is None:
        acc = acc_ref[...]
        return acc[:, :HEAD_DIM] / acc[:, HEAD_DIM:]
    return acc_ref[...] / jnp.sum(l_ref[...], axis=-1, keepdims=True)


def _flash_tiles(scores, update, n_full, diag_mask, s_ref):
    update(n_full, [jnp.where(diag_mask, s, NEG) for s in scores(n_full)])
    chains = s_ref.shape[1]

    def put(slot, tiles):
        for c, s in enumerate(tiles):
            s_ref[slot, c] = s

    def get(slot):
        return [s_ref[slot, c] for c in range(chains)]

    @pl.when(n_full > 0)
    def _():
        put(0, scores(0))

        def body(j, carry):
            cur = j & 1
            tiles = get(cur)
            put(1 - cur, scores(j + 1))
            update(j, tiles)
            return carry

        lax.fori_loop(0, n_full - 1, body, 0)
        last = n_full - 1
        update(last, get(last & 1))


def _causal_mask(rows, tq, tk, q0, k0):
    row = lax.broadcasted_iota(jnp.int32, (rows, tk), 0) & (tq - 1)
    col = lax.broadcasted_iota(jnp.int32, (rows, tk), 1)
    return k0 + col <= q0 + row


def _diff_kernel(lam_ref, sg_ref, q_ref, k_ref, v_ref, o_ref, vext_ref, m_ref, acc_ref, s_ref,
                 *, lam_init, tk, heads):
    i = pl.program_id(2)
    tq = q_ref.shape[0]
    seq = k_ref.shape[0]
    rows = 2 * tq

    @pl.when(i == 0)
    def _():
        for hh in range(heads):
            vext_ref[hh, :, :HEAD_DIM] = v_ref[:, hh * HEAD_DIM:(hh + 1) * HEAD_DIM]
            vext_ref[hh, :, HEAD_DIM:] = jnp.ones((seq, HEAD_DIM), BF16)

    q2 = []
    for hh in range(heads):
        q = q_ref[:, hh * HEAD_DIM:(hh + 1) * HEAD_DIM]
        lane = lax.broadcasted_iota(jnp.int32, q.shape, 1)
        zero = jnp.zeros_like(q)
        q2.append(jnp.concatenate([jnp.where(lane < DIFF_QK_DIM, q, zero),
                                   jnp.where(lane >= DIFF_QK_DIM, q, zero)], axis=0))
    m_ref[...] = jnp.full_like(m_ref, NEG)
    acc_ref[...] = jnp.zeros_like(acc_ref)

    def scores(j):
        start = pl.multiple_of(j * tk, tk)
        return [_dot_nt(q2[hh], k_ref[pl.ds(start, tk), hh * HEAD_DIM:(hh + 1) * HEAD_DIM])
                for hh in range(heads)]

    def update(j, tiles):
        start = pl.multiple_of(j * tk, tk)
        for hh in range(heads):
            _softmax_update(tiles[hh], vext_ref[hh, pl.ds(start, tk), :], m_ref.at[hh], None, acc_ref.at[hh])

    n_full = _div_pow2(i * tq, tk)
    _flash_tiles(scores, update, n_full, _causal_mask(rows, tq, tk, i * tq, n_full * tk), s_ref)

    lam_p = lam_ref[...]
    lam = (jnp.exp(jnp.sum(lam_p[0:1] * lam_p[1:2], axis=-1, keepdims=True))
           - jnp.exp(jnp.sum(lam_p[2:3] * lam_p[3:4], axis=-1, keepdims=True)) + lam_init)
    for hh in range(heads):
        o_all = _softmax_finish(None, acc_ref.at[hh])
        o = o_all[:tq] - lam * o_all[tq:]
        o = o * lax.rsqrt(jnp.mean(o * o, axis=-1, keepdims=True) + RMS_EPS) * sg_ref[...]
        o_ref[:, hh * HEAD_DIM:(hh + 1) * HEAD_DIM] = (o * (1.0 - lam_init)).astype(o_ref.dtype)


def _diff_attention(proj, lam_p, subln_g, batch, seq, n_heads, cols, lam_init):
    tq = _tile(seq, 256)
    tk = _tile(seq, 1024)
    heads = 2
    assert n_heads % heads == 0 and tk % tq == 0
    nq = seq // tq
    wide = heads * HEAD_DIM
    k0 = cols["dk"] * 2 // heads
    v0 = cols["dv"] * 2 // heads
    kv_spec = lambda c0: pl.BlockSpec((seq, wide), lambda b, h, i: (b, c0 + h))
    return pl.pallas_call(
        functools.partial(_diff_kernel, lam_init=lam_init, tk=tk, heads=heads),
        grid=(batch, n_heads // heads, nq),
        in_specs=[
            pl.BlockSpec((4, DIFF_QK_DIM), lambda b, h, i: (0, 0)),
            pl.BlockSpec((1, HEAD_DIM), lambda b, h, i: (0, 0)),
            pl.BlockSpec((tq, wide), lambda b, h, i: (b * nq + i, h)),
            kv_spec(k0),
            kv_spec(v0),
        ],
        out_specs=pl.BlockSpec((tq, wide), lambda b, h, i: (b * nq + i, h)),
        out_shape=jax.ShapeDtypeStruct((batch * seq, n_heads * HEAD_DIM), BF16),
        scratch_shapes=[pltpu.VMEM((heads, seq, 2 * HEAD_DIM), BF16),
                        pltpu.VMEM((heads, 2 * tq, 1), F32),
                        pltpu.VMEM((heads, 2 * tq, 2 * HEAD_DIM), F32),
                        pltpu.VMEM((2, heads, 2 * tq, tk), F32)],
        compiler_params=_cparams("arbitrary", "arbitrary", "arbitrary"),
        name="diff_attention",
    )(lam_p, subln_g.reshape(1, HEAD_DIM), proj, proj, proj)


def _cmp_kernel(x_ref, pe_ref, w1_ref, w2_ref, o_ref):
    x = x_ref[0, 0, 0]
    half = x.shape[1]
    xa = (x + pe_ref[0, 0:1, :]).astype(BF16)
    xb = (x + pe_ref[0, 1:2, :]).astype(BF16)
    ya = _dot(xa, w1_ref[0, :half, :].astype(BF16))
    yb = _dot(xb, w1_ref[0, half:, :].astype(BF16))
    nrow = x.shape[0]
    row = lax.broadcasted_iota(jnp.int32, yb.shape, 0)
    yb_next = jnp.where(row < nrow - 1, pltpu.roll(yb, nrow - 1, 0), 0.0)
    hid = _silu(ya + yb_next).astype(BF16)
    o_ref[0, 0, 0] = _dot(hid, w2_ref[0].astype(BF16))


def _compress(xt, pe2, w1, w2):
    _, batch, groups, nrow, wide = xt.shape
    hidden = w1.shape[-1]
    return pl.pallas_call(
        _cmp_kernel,
        grid=(2, batch, groups),
        in_specs=[
            pl.BlockSpec((1, 1, 1, nrow, wide), lambda j, b, g: (j, b, g, 0, 0)),
            pl.BlockSpec((1, 2, wide), lambda j, b, g: (j, 0, 0)),
            pl.BlockSpec((1, 2 * wide, hidden), lambda j, b, g: (j, 0, 0)),
            pl.BlockSpec((1, hidden, HEAD_DIM), lambda j, b, g: (j, 0, 0)),
        ],
        out_specs=pl.BlockSpec((1, 1, 1, nrow, HEAD_DIM), lambda j, b, g: (j, b, g, 0, 0)),
        out_shape=jax.ShapeDtypeStruct((2, batch, groups, nrow, HEAD_DIM), F32),
        compiler_params=_cparams("arbitrary", "arbitrary", "arbitrary"),
        name="nsa_compress",
    )(xt, pe2, w1, w2)


def _cmpattn_kernel(q_ref, kc_ref, vc_ref, oc_ref, sel_ref, score_ref, *, n_cmp, n_sel, hpg):
    i = pl.program_id(2)
    tq = q_ref.shape[0]
    kc = kc_ref[0, 0, 0].astype(BF16)
    vc = vc_ref[0, 0, 0].astype(BF16)
    ncp = kc.shape[0]
    tpos = i * tq + lax.broadcasted_iota(jnp.int32, (tq, ncp), 0)
    nidx = lax.broadcasted_iota(jnp.int32, (tq, ncp), 1)
    cmask = (nidx * NSA_CMP_STRIDE + NSA_CMP_LEN - 1 <= tpos) & (nidx < n_cmp)
    psum = jnp.zeros((tq, ncp), F32)
    for j in range(hpg):
        qj = q_ref[:, j * HEAD_DIM:(j + 1) * HEAD_DIM]
        s = jnp.where(cmask, _dot_nt(qj, kc), NEG)
        m = jnp.max(s, axis=-1, keepdims=True)
        e = jnp.where(cmask, jnp.exp(s - m), 0.0)
        den = jnp.sum(e, axis=-1, keepdims=True)
        p = e / jnp.where(den > 0.0, den, 1.0)
        oc_ref[:, j * HEAD_DIM:(j + 1) * HEAD_DIM] = _dot(p.astype(BF16), vc)
        psum = psum + p

    sm = lax.broadcasted_iota(jnp.int32, (n_sel, ncp), 0) * NSA_SEL_BLOCK
    cn = lax.broadcasted_iota(jnp.int32, (n_sel, ncp), 1) * NSA_CMP_STRIDE
    ov = jnp.maximum(jnp.minimum(cn + NSA_CMP_LEN, sm + NSA_SEL_BLOCK) - jnp.maximum(cn, sm), 0)
    ov = (ov.astype(F32) / NSA_CMP_LEN).astype(BF16)
    p_hi = psum.astype(BF16)
    p_lo = (psum - p_hi.astype(F32)).astype(BF16)
    imp = _dot_nt(ov, p_hi) + _dot_nt(ov, p_lo)

    t = i * tq + lax.broadcasted_iota(jnp.int32, (n_sel, tq), 1)
    blk = lax.broadcasted_iota(jnp.int32, (n_sel, tq), 0)
    cur = _div_pow2(t, NSA_SEL_BLOCK)
    valid = blk * NSA_SEL_BLOCK <= t
    forced = (blk == 0) | (blk == cur) | (blk == cur - 1)
    score = jnp.where(valid, imp + jnp.where(forced, FORCE_BONUS, 0.0), NEG)
    score_ref[...] = score
    rank = jnp.zeros((n_sel, tq), F32)
    for c in range(n_sel):
        other = score_ref[c:c + 1, :]
        ahead = (other > score) | ((other == score) & (blk > c))
        rank = rank + jnp.where(ahead, 1.0, 0.0)
    n_top = min(NSA_TOP_N, n_sel)
    sel_t = jnp.where((rank < n_top) & valid, 1.0, 0.0)
    sel_t = jnp.concatenate([sel_t, jnp.zeros((V7X_LANES - n_sel, tq), F32)], axis=0).astype(BF16)
    eye = (lax.broadcasted_iota(jnp.int32, (tq, tq), 0) == lax.broadcasted_iota(jnp.int32, (tq, tq), 1))
    sel_ref[0, 0] = _dot_nt(jnp.where(eye, 1.0, 0.0).astype(BF16), sel_t).astype(sel_ref.dtype)


def _compressed_attention(proj, kvc, batch, seq, groups, hpg, cols):
    tq = _tile(seq, 256)
    nq = seq // tq
    ncp = seq // NSA_CMP_STRIDE
    n_cmp = (seq - NSA_CMP_LEN) // NSA_CMP_STRIDE + 1
    n_sel = seq // NSA_SEL_BLOCK
    wq = hpg * HEAD_DIM
    q0 = cols["nq"] * 2 * V7X_LANES // wq
    kv_spec = lambda j: pl.BlockSpec((1, 1, 1, ncp, HEAD_DIM), lambda b, g, i: (j, b, g, 0, 0))
    return pl.pallas_call(
        functools.partial(_cmpattn_kernel, n_cmp=n_cmp, n_sel=n_sel, hpg=hpg),
        grid=(batch, groups, nq),
        in_specs=[
            pl.BlockSpec((tq, wq), lambda b, g, i: (b * nq + i, q0 + g)),
            kv_spec(0),
            kv_spec(1),
        ],
        out_specs=[
            pl.BlockSpec((tq, wq), lambda b, g, i: (b * nq + i, g)),
            pl.BlockSpec((1, 1, tq, V7X_LANES), lambda b, g, i: (b, g, i, 0)),
        ],
        out_shape=[
            jax.ShapeDtypeStruct((batch * seq, groups * wq), F32),
            jax.ShapeDtypeStruct((batch, groups, seq, V7X_LANES), BF16),
        ],
        scratch_shapes=[pltpu.VMEM((n_sel, tq), F32)],
        compiler_params=_cparams("arbitrary", "arbitrary", "arbitrary"),
        name="nsa_compressed_attention",
    )(proj, kvc, kvc)


def _sel_kernel(q_ref, ks_ref, vs_ref, kw_ref, vw_ref, sel_ref, oc_ref, gl_ref, o_ref,
                kaug_ref, m_ref, l_ref, acc_ref, s_ref, *, hpg, tk, n_sel, g_axis):
    i = pl.program_id(2)
    g = pl.program_id(g_axis)
    tq = q_ref.shape[0]
    seq = ks_ref.shape[0]
    rows = hpg * tq
    q0 = i * tq

    @pl.when(i == 0)
    def _():
        kaug_ref[:, :HEAD_DIM] = ks_ref[...]
        kpos = lax.broadcasted_iota(jnp.int32, (seq, V7X_LANES), 0)
        lane = lax.broadcasted_iota(jnp.int32, (seq, V7X_LANES), 1)
        kaug_ref[:, HEAD_DIM:] = jnp.where(_div_pow2(kpos, NSA_SEL_BLOCK) == lane, 1.0, 0.0).astype(BF16)

    q_heads = [q_ref[:, j * HEAD_DIM:(j + 1) * HEAD_DIM] for j in range(hpg)]
    q4 = jnp.concatenate(q_heads, axis=0)
    lane = lax.broadcasted_iota(jnp.int32, (tq, V7X_LANES), 1)
    unsel = (sel_ref[0, 0].astype(F32) < 0.5) & (lane < n_sel)
    block_bias = jnp.where(unsel, NEG, 0.0).astype(BF16)
    q_aug = jnp.concatenate([jnp.concatenate([qj, block_bias], axis=1) for qj in q_heads], axis=0)

    m_ref[...] = jnp.full_like(m_ref, NEG)
    l_ref[...] = jnp.zeros_like(l_ref)
    acc_ref[...] = jnp.zeros_like(acc_ref)

    def scores(j):
        return [_dot_nt(q_aug, kaug_ref[pl.ds(pl.multiple_of(j * tk, tk), tk), :])]

    def update(j, tiles):
        _softmax_update(tiles[0], vs_ref[pl.ds(pl.multiple_of(j * tk, tk), tk), :], m_ref, l_ref, acc_ref)

    n_full = _div_pow2(q0, tk)
    _flash_tiles(scores, update, n_full, _causal_mask(rows, tq, tk, q0, n_full * tk), s_ref)
    o_s = _softmax_finish(l_ref, acc_ref)

    span = NSA_WINDOW + tq
    wstart = pl.multiple_of(jnp.maximum(q0 - NSA_WINDOW, 0), tq)
    kt = kw_ref[pl.ds(wstart, span), :]
    vt = vw_ref[pl.ds(wstart, span), :]
    qpos = q0 + lax.broadcasted_iota(jnp.int32, (tq, span), 0)
    kcol = wstart + lax.broadcasted_iota(jnp.int32, (tq, span), 1)
    dist = qpos - kcol
    wbias = jnp.where((dist >= 0) & (dist < NSA_WINDOW), 0.0, NEG)
    s = _dot_nt(q4, kt) + jnp.concatenate([wbias] * hpg, axis=0)
    m = jnp.max(s, axis=-1, keepdims=True)
    p = jnp.exp(s - m)
    o_w = _dot(p.astype(BF16), vt) / jnp.sum(p, axis=-1, keepdims=True)

    gates = jax.nn.sigmoid(gl_ref[...])
    lane = lax.broadcasted_iota(jnp.int32, gates.shape, 1)
    for j in range(hpg):
        head = g * hpg + j
        rows = slice(j * tq, (j + 1) * tq)
        gsel = [jnp.sum(jnp.where(lane == 3 * head + c, gates, 0.0), axis=-1, keepdims=True) for c in range(3)]
        o = (gsel[0] * oc_ref[:, j * HEAD_DIM:(j + 1) * HEAD_DIM] + gsel[1] * o_s[rows] + gsel[2] * o_w[rows])
        o_ref[:, j * HEAD_DIM:(j + 1) * HEAD_DIM] = o.astype(o_ref.dtype)


def _selected_window_attention(proj, sel, o_c, gl, batch, seq, groups, hpg, cols):
    tq = _tile(seq, 128)
    nq = seq // tq
    n_sel = seq // NSA_SEL_BLOCK
    wq = hpg * HEAD_DIM
    q0 = cols["nq"] * 2 * V7X_LANES // wq
    kv_spec = lambda name, off: pl.BlockSpec(
        (seq, HEAD_DIM), lambda b, g, i: (b, (cols[name] * 2 + off * groups) + g))
    tk = _tile(seq, 1024)
    assert tk % tq == 0 and n_sel <= V7X_LANES and n_sel % 8 == 0
    return pl.pallas_call(
        functools.partial(_sel_kernel, hpg=hpg, tk=tk, n_sel=n_sel, g_axis=1),
        grid=(batch, groups, nq),
        in_specs=[
            pl.BlockSpec((tq, wq), lambda b, g, i: (b * nq + i, q0 + g)),
            kv_spec("ks", 0), kv_spec("ks", 1), kv_spec("kw", 0), kv_spec("kw", 1),
            pl.BlockSpec((1, 1, tq, V7X_LANES), lambda b, g, i: (b, g, i, 0)),
            pl.BlockSpec((tq, wq), lambda b, g, i: (b * nq + i, g)),
            pl.BlockSpec((tq, V7X_LANES), lambda b, g, i: (b * nq + i, 0)),
        ],
        out_specs=pl.BlockSpec((tq, wq), lambda b, g, i: (b * nq + i, g)),
        out_shape=jax.ShapeDtypeStruct((batch * seq, groups * wq), BF16),
        scratch_shapes=[pltpu.VMEM((seq, 2 * HEAD_DIM), BF16),
                        pltpu.VMEM((hpg * tq, 1), F32), pltpu.VMEM((hpg * tq, HEAD_DIM), F32),
                        pltpu.VMEM((hpg * tq, HEAD_DIM), F32),
                        pltpu.VMEM((2, 1, hpg * tq, tk), F32)],
        compiler_params=_cparams("arbitrary", "arbitrary", "arbitrary"),
        name="nsa_selected_window",
    )(proj, proj, proj, proj, proj, sel, o_c, gl)


def _out_kernel(od_ref, on_ref, w_ref, x_ref, mod_ref, o_ref, *, gate_idx, rc):
    kd = od_ref.shape[1]
    w = w_ref[...].astype(BF16)
    gate = mod_ref[0, gate_idx:gate_idx + 1, :]
    for r in range(od_ref.shape[0] // rc):
        rows = slice(r * rc, (r + 1) * rc)
        y = _dot(od_ref[rows, :], w[:kd]) + _dot(on_ref[rows, :], w[kd:])
        o_ref[rows, :] = x_ref[rows, :] + gate * y


def _output_projection(o_diff, o_nsa, w_o, l, x, mod_l, seq, gate_idx):
    t, d = x.shape
    kd, kn = o_diff.shape[1], o_nsa.shape[1]
    tm = _tile(seq, 1024)
    tn = _tile(d, 512)
    per_b = seq // tm
    return pl.pallas_call(
        functools.partial(_out_kernel, gate_idx=gate_idx, rc=_tile(tm, 512)),
        grid=(t // tm, d // tn),
        in_specs=[
            pl.BlockSpec((tm, kd), lambda i, n: (i, 0)),
            pl.BlockSpec((tm, kn), lambda i, n: (i, 0)),
            pl.BlockSpec((None, kd + kn, tn), lambda i, n: (l, 0, n)),
            pl.BlockSpec((tm, tn), lambda i, n: (i, n)),
            pl.BlockSpec((1, N_MOD, tn), lambda i, n: (i // per_b, 0, n)),
        ],
        out_specs=pl.BlockSpec((tm, tn), lambda i, n: (i, n)),
        out_shape=jax.ShapeDtypeStruct((t, d), F32),
        compiler_params=_cparams("arbitrary", "arbitrary"),
        name="output_projection",
    )(o_diff, o_nsa, w_o, x, mod_l)


def kernel(x, c, w_ada, b_ada, norm_g, ffn_w_gu, ffn_w_d, w_in, w_o, diff_lam, diff_subln,
           cmp_pe, cmp_w1, cmp_w2, final_g):
    batch, seq, d = x.shape
    depth = w_ada.shape[0]
    n_diff = d // (2 * HEAD_DIM)
    n_nsa = d // (2 * HEAD_DIM)
    groups = N_NSA_KV
    hpg = n_nsa // groups
    diff_w = n_diff * HEAD_DIM
    nsa_w = n_nsa * HEAD_DIM
    kv_w = groups * HEAD_DIM
    n_main = 3 * diff_w + nsa_w + 6 * kv_w
    assert w_in.shape[-1] == n_main + 3 * n_nsa
    cw = 2 * V7X_LANES
    assert diff_w % cw == 0 and nsa_w % cw == 0 and kv_w == cw
    assert seq >= NSA_WINDOW + 128
    cols = {"dk": diff_w // cw, "dv": 2 * diff_w // cw, "nq": 3 * diff_w // cw}
    cols["kc"] = cols["nq"] + nsa_w // cw
    cols.update(vc=cols["kc"] + 1, ks=cols["kc"] + 2, vs=cols["kc"] + 3, kw=cols["kc"] + 4, vw=cols["kc"] + 5)

    t = batch * seq
    xf = x.reshape(t, d)
    c_pad = jnp.pad(c, ((0, 8 - batch), (0, 0)))
    mod = _modulation(c_pad, w_ada, b_ada).reshape(depth, 8, N_MOD, d)
    tabs = (_rope_tables(seq, DIFF_QK_DIM // ROPE_FRACTION, DIFF_QK_DIM)
            + _rope_tables(seq, HEAD_DIM // ROPE_FRACTION, HEAD_DIM))
    ncp = seq // NSA_CMP_STRIDE
    cmp_half = NSA_CMP_STRIDE * HEAD_DIM

    for l in range(depth):
        mod_l = mod[l]
        lam_init = 0.8 - 0.6 * math.exp(-0.3 * l)

        h = _norm(xf, norm_g[l, 0], mod_l, seq, 0, 1, BF16)
        a = _ffn_gate_up(h, ffn_w_gu, l, 0)
        xf = _gated_residual_matmul(a, ffn_w_d, (l, 0), xf, mod_l, seq, 2, 0.5)

        h = _norm(xf, norm_g[l, 1], mod_l, seq, 3, 4, BF16)
        w_gate = jnp.pad(w_in[l, :, n_main:], ((0, 0), (0, V7X_LANES - 3 * n_nsa)))
        proj, kvc_raw, gl = _input_projection(h, w_in, l, w_gate, tabs, seq, cols, n_main)
        o_diff = _diff_attention(proj, diff_lam[l], diff_subln[l], batch, seq, n_diff, cols, lam_init)
        xt = kvc_raw.reshape(batch, ncp, NSA_CMP_STRIDE, 2, groups, HEAD_DIM)
        xt = xt.transpose(3, 0, 4, 1, 2, 5).reshape(2, batch, groups, ncp, cmp_half)
        pe2 = cmp_pe[l].reshape(2, 2, cmp_half)
        kvc = _compress(xt, pe2, cmp_w1[l], cmp_w2[l])
        o_c, sel = _compressed_attention(proj, kvc, batch, seq, groups, hpg, cols)
        o_nsa = _selected_window_attention(proj, sel, o_c, gl, batch, seq, groups, hpg, cols)
        xf = _output_projection(o_diff, o_nsa, w_o, l, xf, mod_l, seq, 5)

        h = _norm(xf, norm_g[l, 2], mod_l, seq, 6, 7, BF16)
        a = _ffn_gate_up(h, ffn_w_gu, l, 1)
        xf = _gated_residual_matmul(a, ffn_w_d, (l, 1), xf, mod_l, seq, 8, 0.5)

    out = _norm(xf, final_g, mod[0], seq, None, None, F32)
    return out.reshape(batch, seq, d)
```

```python
import functools
import math

import jax
import jax.numpy as jnp
from jax import lax
from jax.experimental import pallas as pl
from jax.experimental.pallas import tpu as pltpu

F32 = jnp.float32
BF16 = jnp.bfloat16

HEAD_DIM = 128
DIFF_QK_DIM = HEAD_DIM // 2
N_NSA_KV = 2
NSA_CMP_LEN = 32
NSA_CMP_STRIDE = 16
NSA_SEL_BLOCK = 64
NSA_TOP_N = 16
NSA_WINDOW = 512
ROPE_THETA = 500000.0
ROPE_FRACTION = 4
N_MOD = 9
RMS_EPS = 1e-6
NEG = -1e30
FORCE_BONUS = 1e6

V7X_LANES = 128
V7X_VMEM_LIMIT_BYTES = 56 * 1024 * 1024


def _cparams(*sem):
    return pltpu.CompilerParams(dimension_semantics=sem, vmem_limit_bytes=V7X_VMEM_LIMIT_BYTES)


def _tile(n, pref):
    t = min(pref, n)
    while n % t:
        t //= 2
    return t


def _dot(a, b):
    return jnp.dot(a, b, preferred_element_type=F32)


def _dot_nt(a, b):
    return lax.dot_general(a, b, (((1,), (1,)), ((), ())), preferred_element_type=F32)


def _silu(x):
    return x * jax.nn.sigmoid(x)


def _div_pow2(x, n):
    assert n & (n - 1) == 0
    return jnp.right_shift(x, n.bit_length() - 1)


def _mod_kernel(c_ref, w_ref, b_ref, o_ref):
    cs = _silu(c_ref[...]).astype(BF16)
    o_ref[0] = _dot(cs, w_ref[0].astype(BF16)) + b_ref[0]


def _modulation(c_pad, w_ada, b_ada):
    depth, d, n9 = w_ada.shape
    tn = _tile(n9, 1024)
    return pl.pallas_call(
        _mod_kernel,
        grid=(depth, n9 // tn),
        in_specs=[
            pl.BlockSpec((8, d), lambda l, n: (0, 0)),
            pl.BlockSpec((1, d, tn), lambda l, n: (l, 0, n)),
            pl.BlockSpec((1, 1, tn), lambda l, n: (l, 0, n)),
        ],
        out_specs=pl.BlockSpec((1, 8, tn), lambda l, n: (l, 0, n)),
        out_shape=jax.ShapeDtypeStruct((depth, 8, n9), F32),
        compiler_params=_cparams("arbitrary", "arbitrary"),
        name="adaln_mod",
    )(c_pad, w_ada, b_ada.reshape(depth, 1, n9))


def _norm_kernel(x_ref, g_ref, mod_ref, o_ref, *, shift_idx, scale_idx):
    x = x_ref[...]
    y = x * lax.rsqrt(jnp.mean(x * x, axis=-1, keepdims=True) + RMS_EPS) * g_ref[...]
    if scale_idx is not None:
        y = y * (1.0 + mod_ref[0, scale_idx:scale_idx + 1, :]) + mod_ref[0, shift_idx:shift_idx + 1, :]
    o_ref[...] = y.astype(o_ref.dtype)


def _norm(x, g, mod_l, seq, shift_idx, scale_idx, out_dtype):
    t, d = x.shape
    tr = _tile(seq, 512)
    per_b = seq // tr
    return pl.pallas_call(
        functools.partial(_norm_kernel, shift_idx=shift_idx, scale_idx=scale_idx),
        grid=(t // tr,),
        in_specs=[
            pl.BlockSpec((tr, d), lambda i: (i, 0)),
            pl.BlockSpec((1, d), lambda i: (0, 0)),
            pl.BlockSpec((1, N_MOD, d), lambda i: (i // per_b, 0, 0)),
        ],
        out_specs=pl.BlockSpec((tr, d), lambda i: (i, 0)),
        out_shape=jax.ShapeDtypeStruct((t, d), out_dtype),
        compiler_params=_cparams("arbitrary"),
        name="rms_adaln",
    )(x, g.reshape(1, d), mod_l)


def _gu_kernel(h_ref, wg_ref, wu_ref, o_ref, *, rc):
    wg = wg_ref[...].astype(BF16)
    wu = wu_ref[...].astype(BF16)
    for r in range(h_ref.shape[0] // rc):
        rows = slice(r * rc, (r + 1) * rc)
        h = h_ref[rows, :]
        g = _dot(h, wg)
        u = _dot(h, wu)
        o_ref[rows, :] = (_silu(g) * u).astype(BF16)


def _ffn_gate_up(h, w_gu, l, k):
    t, d = h.shape
    dff = w_gu.shape[-1] // 2
    tm = _tile(t, 2048)
    tf = _tile(dff, 256)
    nf = dff // tf
    return pl.pallas_call(
        functools.partial(_gu_kernel, rc=_tile(tm, 512)),
        grid=(t // tm, nf),
        in_specs=[
            pl.BlockSpec((tm, d), lambda i, f: (i, 0)),
            pl.BlockSpec((None, None, d, tf), lambda i, f: (l, k, 0, f)),
            pl.BlockSpec((None, None, d, tf), lambda i, f: (l, k, 0, f + nf)),
        ],
        out_specs=pl.BlockSpec((tm, tf), lambda i, f: (i, f)),
        out_shape=jax.ShapeDtypeStruct((t, dff), BF16),
        compiler_params=_cparams("arbitrary", "arbitrary"),
        name="ffn_gate_up",
    )(h, w_gu, w_gu)


def _down_kernel(a_ref, w_ref, x_ref, mod_ref, o_ref, *, gate_idx, coef, rc):
    k = pl.program_id(2)
    w = w_ref[...].astype(BF16)

    @pl.when(k == 0)
    def _():
        o_ref[...] = jnp.zeros_like(o_ref)

    for r in range(a_ref.shape[0] // rc):
        rows = slice(r * rc, (r + 1) * rc)
        o_ref[rows, :] += _dot(a_ref[rows, :], w)

    @pl.when(k == pl.num_programs(2) - 1)
    def _():
        gate = coef * mod_ref[0, gate_idx:gate_idx + 1, :]
        o_ref[...] = x_ref[...] + gate * o_ref[...]


def _gated_residual_matmul(a, w_full, w_prefix, x, mod_l, seq, gate_idx, coef):
    t, kdim = a.shape
    d = x.shape[1]
    tm = _tile(seq, 2048)
    tn = _tile(d, 1024)
    tk = _tile(kdim, 512)
    per_b = seq // tm
    npre = len(w_prefix)
    return pl.pallas_call(
        functools.partial(_down_kernel, gate_idx=gate_idx, coef=coef, rc=_tile(tm, 512)),
        grid=(t // tm, d // tn, kdim // tk),
        in_specs=[
            pl.BlockSpec((tm, tk), lambda i, n, k: (i, k)),
            pl.BlockSpec((None,) * npre + (tk, tn), lambda i, n, k: w_prefix + (k, n)),
            pl.BlockSpec((tm, tn), lambda i, n, k: (i, n)),
            pl.BlockSpec((1, N_MOD, tn), lambda i, n, k: (i // per_b, 0, n)),
        ],
        out_specs=pl.BlockSpec((tm, tn), lambda i, n, k: (i, n)),
        out_shape=jax.ShapeDtypeStruct((t, d), F32),
        compiler_params=_cparams("arbitrary", "arbitrary", "arbitrary"),
        name="gated_residual_matmul",
    )(a, w_full, x, mod_l)


def _rope_tile(x, cos, sin_signed, half, period):
    lane = lax.broadcasted_iota(jnp.int32, x.shape, 1)
    first = (lane & (period - 1)) < half
    partner = jnp.where(first, pltpu.roll(x, V7X_LANES - half, 1), pltpu.roll(x, half, 1))
    return x * cos + partner * sin_signed


def _in_kernel(h_ref, w_ref, wgate_ref, cd_ref, sd_ref, cn_ref, sn_ref,
               o_ref, okv_ref, og_ref, *, cols, rc):
    n = pl.program_id(1)
    tm = h_ref.shape[0]
    w = w_ref[...].astype(BF16)
    chunks = [slice(r * rc, (r + 1) * rc) for r in range(tm // rc)]

    @pl.when(n == 0)
    def _():
        wg = wgate_ref[...].astype(BF16)
        for rows in chunks:
            og_ref[rows, :] = _dot(h_ref[rows, :], wg)

    is_dq = n < cols["dk"]
    is_diff = n < cols["dv"]
    is_nq = (n >= cols["nq"]) & (n < cols["kc"])
    is_nsa = is_nq | (n == cols["ks"]) | (n == cols["kw"])
    is_kvc = (n >= cols["kc"]) & (n < cols["ks"])

    def project(finish):
        for rows in chunks:
            finish(rows, _dot(h_ref[rows, :], w))

    def rope(cos_ref, sin_ref, half, period, scale):
        def finish(rows, y):
            c = cos_ref[rows, :]
            s = sin_ref[rows, :]
            halves = [_rope_tile(y[:, j * V7X_LANES:(j + 1) * V7X_LANES], c, s, half, period)
                      for j in range(y.shape[1] // V7X_LANES)]
            o_ref[rows, :] = (jnp.concatenate(halves, axis=1) * scale).astype(BF16)
        return finish

    @pl.when(is_diff)
    def _():
        project(rope(cd_ref, sd_ref, DIFF_QK_DIM // ROPE_FRACTION // 2, DIFF_QK_DIM,
                     jnp.where(is_dq, DIFF_QK_DIM ** -0.5, 1.0)))

    @pl.when(is_nsa)
    def _():
        project(rope(cn_ref, sn_ref, HEAD_DIM // ROPE_FRACTION // 2, HEAD_DIM,
                     jnp.where(is_nq, HEAD_DIM ** -0.5, 1.0)))

    @pl.when(is_kvc)
    def _():
        def finish(rows, y):
            o_ref[rows, :] = y.astype(BF16)
            for g in range(okv_ref.shape[2]):
                okv_ref[0, 0, g, rows, :] = y[:, g * HEAD_DIM:(g + 1) * HEAD_DIM]
        project(finish)

    @pl.when(jnp.logical_not(is_diff | is_nsa | is_kvc))
    def _():
        def finish(rows, y):
            o_ref[rows, :] = y.astype(BF16)
        project(finish)


def _input_projection(h, w_in, l, w_gate, tabs, seq, cols, n_main):
    t, d = h.shape
    tm = _tile(seq, 2048)
    tn = 2 * V7X_LANES
    per_b = seq // tm
    nb = n_main // tn
    kc0 = cols["kc"]
    tab_spec = pl.BlockSpec((tm, V7X_LANES), lambda i, n: (i % per_b, 0))
    return pl.pallas_call(
        functools.partial(_in_kernel, cols=cols, rc=_tile(tm, 512)),
        grid=(t // tm, nb),
        in_specs=[
            pl.BlockSpec((tm, d), lambda i, n: (i, 0)),
            pl.BlockSpec((None, d, tn), lambda i, n: (l, 0, n)),
            pl.BlockSpec((d, V7X_LANES), lambda i, n: (0, 0)),
            tab_spec, tab_spec, tab_spec, tab_spec,
        ],
        out_specs=[
            pl.BlockSpec((tm, tn), lambda i, n: (i, n)),
            pl.BlockSpec((1, 1, N_NSA_KV, tm, HEAD_DIM),
                         lambda i, n: (jnp.clip(n - kc0, 0, 1), i // per_b, 0, i % per_b, 0)),
            pl.BlockSpec((tm, V7X_LANES), lambda i, n: (i, 0)),
        ],
        out_shape=[
            jax.ShapeDtypeStruct((t, n_main), BF16),
            jax.ShapeDtypeStruct((2, t // seq, N_NSA_KV, seq, HEAD_DIM), F32),
            jax.ShapeDtypeStruct((t, V7X_LANES), F32),
        ],
        compiler_params=_cparams("arbitrary", "arbitrary"),
        name="input_projection",
    )(h, w_in, w_gate, *tabs)


def _rope_tables(seq, rot_dim, period):
    half = rot_dim // 2
    inv = jnp.exp(-math.log(ROPE_THETA) * jnp.arange(half, dtype=F32) / half)
    ang = jnp.arange(seq, dtype=F32)[:, None] * inv[None, :]
    cos, sin = jnp.cos(ang), jnp.sin(ang)
    pad = jnp.zeros((seq, period - rot_dim), F32)
    cos_p = jnp.concatenate([cos, cos, pad + 1.0], axis=1)
    sin_p = jnp.concatenate([-sin, sin, pad], axis=1)
    reps = V7X_LANES // period
    return jnp.tile(cos_p, (1, reps)), jnp.tile(sin_p, (1, reps))


def _softmax_update(s, v_tile, m_ref, l_ref, acc_ref):
    m_old = m_ref[...]
    m_new = jnp.maximum(m_old, jnp.max(s, axis=-1, keepdims=True))
    alpha = jnp.exp(m_old - m_new)
    p = jnp.exp(s - m_new)
    if l_ref is not None:
        part = p[:, :V7X_LANES]
        for c in range(1, p.shape[1] // V7X_LANES):
            part = part + p[:, c * V7X_LANES:(c + 1) * V7X_LANES]
        l_ref[...] = alpha * l_ref[...] + part
    acc_ref[...] = alpha * acc_ref[...] + _dot(p.astype(BF16), v_tile)
    m_ref[...] = m_new


def _softmax_finish(l_ref, acc_ref):
    if l_ref is None:
        acc = acc_ref[...]
        return acc[:, :HEAD_DIM] / acc[:, HEAD_DIM:]
    return acc_ref[...] / jnp.sum(l_ref[...], axis=-1, keepdims=True)


def _flash_tiles(scores, update, n_full, diag_mask, s_ref):
    update(n_full, [jnp.where(diag_mask, s, NEG) for s in scores(n_full)])
    chains = s_ref.shape[1]

    def put(slot, tiles):
        for c, s in enumerate(tiles):
            s_ref[slot, c] = s

    def get(slot):
        return [s_ref[slot, c] for c in range(chains)]

    @pl.when(n_full > 0)
    def _():
        put(0, scores(0))

        def body(j, carry):
            cur = j & 1
            tiles = get(cur)
            put(1 - cur, scores(j + 1))
            update(j, tiles)
            return carry

        lax.fori_loop(0, n_full - 1, body, 0)
        last = n_full - 1
        update(last, get(last & 1))


def _causal_mask(rows, tq, tk, q0, k0):
    row = lax.broadcasted_iota(jnp.int32, (rows, tk), 0) & (tq - 1)
    col = lax.broadcasted_iota(jnp.int32, (rows, tk), 1)
    return k0 + col <= q0 + row


def _diff_kernel(lam_ref, sg_ref, q_ref, k_ref, v_ref, o_ref, vext_ref, m_ref, acc_ref, s_ref,
                 *, lam_init, tk, heads):
    i = pl.program_id(2)
    tq = q_ref.shape[0]
    seq = k_ref.shape[0]
    rows = 2 * tq

    @pl.when(i == 0)
    def _():
        for hh in range(heads):
            vext_ref[hh, :, :HEAD_DIM] = v_ref[:, hh * HEAD_DIM:(hh + 1) * HEAD_DIM]
            vext_ref[hh, :, HEAD_DIM:] = jnp.ones((seq, HEAD_DIM), BF16)

    q2 = []
    for hh in range(heads):
        q = q_ref[:, hh * HEAD_DIM:(hh + 1) * HEAD_DIM]
        lane = lax.broadcasted_iota(jnp.int32, q.shape, 1)
        zero = jnp.zeros_like(q)
        q2.append(jnp.concatenate([jnp.where(lane < DIFF_QK_DIM, q, zero),
                                   jnp.where(lane >= DIFF_QK_DIM, q, zero)], axis=0))
    m_ref[...] = jnp.full_like(m_ref, NEG)
    acc_ref[...] = jnp.zeros_like(acc_ref)

    def scores(j):
        start = pl.multiple_of(j * tk, tk)
        return [_dot_nt(q2[hh], k_ref[pl.ds(start, tk), hh * HEAD_DIM:(hh + 1) * HEAD_DIM])
                for hh in range(heads)]

    def update(j, tiles):
        start = pl.multiple_of(j * tk, tk)
        for hh in range(heads):
            _softmax_update(tiles[hh], vext_ref[hh, pl.ds(start, tk), :], m_ref.at[hh], None, acc_ref.at[hh])

    n_full = _div_pow2(i * tq, tk)
    _flash_tiles(scores, update, n_full, _causal_mask(rows, tq, tk, i * tq, n_full * tk), s_ref)

    lam_p = lam_ref[...]
    lam = (jnp.exp(jnp.sum(lam_p[0:1] * lam_p[1:2], axis=-1, keepdims=True))
           - jnp.exp(jnp.sum(lam_p[2:3] * lam_p[3:4], axis=-1, keepdims=True)) + lam_init)
    for hh in range(heads):
        o_all = _softmax_finish(None, acc_ref.at[hh])
        o = o_all[:tq] - lam * o_all[tq:]
        o = o * lax.rsqrt(jnp.mean(o * o, axis=-1, keepdims=True) + RMS_EPS) * sg_ref[...]
        o_ref[:, hh * HEAD_DIM:(hh + 1) * HEAD_DIM] = (o * (1.0 - lam_init)).astype(o_ref.dtype)


def _diff_attention(proj, lam_p, subln_g, batch, seq, n_heads, cols, lam_init):
    tq = _tile(seq, 256)
    tk = _tile(seq, 1024)
    heads = 2
    assert n_heads % heads == 0 and tk % tq == 0
    nq = seq // tq
    wide = heads * HEAD_DIM
    k0 = cols["dk"] * 2 // heads
    v0 = cols["dv"] * 2 // heads
    kv_spec = lambda c0: pl.BlockSpec((seq, wide), lambda b, h, i: (b, c0 + h))
    return pl.pallas_call(
        functools.partial(_diff_kernel, lam_init=lam_init, tk=tk, heads=heads),
        grid=(batch, n_heads // heads, nq),
        in_specs=[
            pl.BlockSpec((4, DIFF_QK_DIM), lambda b, h, i: (0, 0)),
            pl.BlockSpec((1, HEAD_DIM), lambda b, h, i: (0, 0)),
            pl.BlockSpec((tq, wide), lambda b, h, i: (b * nq + i, h)),
            kv_spec(k0),
            kv_spec(v0),
        ],
        out_specs=pl.BlockSpec((tq, wide), lambda b, h, i: (b * nq + i, h)),
        out_shape=jax.ShapeDtypeStruct((batch * seq, n_heads * HEAD_DIM), BF16),
        scratch_shapes=[pltpu.VMEM((heads, seq, 2 * HEAD_DIM), BF16),
                        pltpu.VMEM((heads, 2 * tq, 1), F32),
                        pltpu.VMEM((heads, 2 * tq, 2 * HEAD_DIM), F32),
                        pltpu.VMEM((2, heads, 2 * tq, tk), F32)],
        compiler_params=_cparams("arbitrary", "arbitrary", "arbitrary"),
        name="diff_attention",
    )(lam_p, subln_g.reshape(1, HEAD_DIM), proj, proj, proj)


def _cmp_kernel(x_ref, pe_ref, w1_ref, w2_ref, o_ref):
    x = x_ref[0, 0, 0]
    half = x.shape[1]
    xa = (x + pe_ref[0, 0:1, :]).astype(BF16)
    xb = (x + pe_ref[0, 1:2, :]).astype(BF16)
    ya = _dot(xa, w1_ref[0, :half, :].astype(BF16))
    yb = _dot(xb, w1_ref[0, half:, :].astype(BF16))
    nrow = x.shape[0]
    row = lax.broadcasted_iota(jnp.int32, yb.shape, 0)
    yb_next = jnp.where(row < nrow - 1, pltpu.roll(yb, nrow - 1, 0), 0.0)
    hid = _silu(ya + yb_next).astype(BF16)
    o_ref[0, 0, 0] = _dot(hid, w2_ref[0].astype(BF16))


def _compress(xt, pe2, w1, w2):
    _, batch, groups, nrow, wide = xt.shape
    hidden = w1.shape[-1]
    return pl.pallas_call(
        _cmp_kernel,
        grid=(2, batch, groups),
        in_specs=[
            pl.BlockSpec((1, 1, 1, nrow, wide), lambda j, b, g: (j, b, g, 0, 0)),
            pl.BlockSpec((1, 2, wide), lambda j, b, g: (j, 0, 0)),
            pl.BlockSpec((1, 2 * wide, hidden), lambda j, b, g: (j, 0, 0)),
            pl.BlockSpec((1, hidden, HEAD_DIM), lambda j, b, g: (j, 0, 0)),
        ],
        out_specs=pl.BlockSpec((1, 1, 1, nrow, HEAD_DIM), lambda j, b, g: (j, b, g, 0, 0)),
        out_shape=jax.ShapeDtypeStruct((2, batch, groups, nrow, HEAD_DIM), F32),
        compiler_params=_cparams("arbitrary", "arbitrary", "arbitrary"),
        name="nsa_compress",
    )(xt, pe2, w1, w2)


def _cmpattn_kernel(q_ref, kc_ref, vc_ref, oc_ref, sel_ref, score_ref, *, n_cmp, n_sel, hpg):
    i = pl.program_id(2)
    tq = q_ref.shape[0]
    kc = kc_ref[0, 0, 0].astype(BF16)
    vc = vc_ref[0, 0, 0].astype(BF16)
    ncp = kc.shape[0]
    tpos = i * tq + lax.broadcasted_iota(jnp.int32, (tq, ncp), 0)
    nidx = lax.broadcasted_iota(jnp.int32, (tq, ncp), 1)
    cmask = (nidx * NSA_CMP_STRIDE + NSA_CMP_LEN - 1 <= tpos) & (nidx < n_cmp)
    psum = jnp.zeros((tq, ncp), F32)
    for j in range(hpg):
        qj = q_ref[:, j * HEAD_DIM:(j + 1) * HEAD_DIM]
        s = jnp.where(cmask, _dot_nt(qj, kc), NEG)
        m = jnp.max(s, axis=-1, keepdims=True)
        e = jnp.where(cmask, jnp.exp(s - m), 0.0)
        den = jnp.sum(e, axis=-1, keepdims=True)
        p = e / jnp.where(den > 0.0, den, 1.0)
        oc_ref[:, j * HEAD_DIM:(j + 1) * HEAD_DIM] = _dot(p.astype(BF16), vc)
        psum = psum + p

    sm = lax.broadcasted_iota(jnp.int32, (n_sel, ncp), 0) * NSA_SEL_BLOCK
    cn = lax.broadcasted_iota(jnp.int32, (n_sel, ncp), 1) * NSA_CMP_STRIDE
    ov = jnp.maximum(jnp.minimum(cn + NSA_CMP_LEN, sm + NSA_SEL_BLOCK) - jnp.maximum(cn, sm), 0)
    ov = (ov.astype(F32) / NSA_CMP_LEN).astype(BF16)
    p_hi = psum.astype(BF16)
    p_lo = (psum - p_hi.astype(F32)).astype(BF16)
    imp = _dot_nt(ov, p_hi) + _dot_nt(ov, p_lo)

    t = i * tq + lax.broadcasted_iota(jnp.int32, (n_sel, tq), 1)
    blk = lax.broadcasted_iota(jnp.int32, (n_sel, tq), 0)
    cur = _div_pow2(t, NSA_SEL_BLOCK)
    valid = blk * NSA_SEL_BLOCK <= t
    forced = (blk == 0) | (blk == cur) | (blk == cur - 1)
    score = jnp.where(valid, imp + jnp.where(forced, FORCE_BONUS, 0.0), NEG)
    score_ref[...] = score
    rank = jnp.zeros((n_sel, tq), F32)
    for c in range(n_sel):
        other = score_ref[c:c + 1, :]
        ahead = (other > score) | ((other == score) & (blk > c))
        rank = rank + jnp.where(ahead, 1.0, 0.0)
    n_top = min(NSA_TOP_N, n_sel)
    sel_t = jnp.where((rank < n_top) & valid, 1.0, 0.0)
    sel_t = jnp.concatenate([sel_t, jnp.zeros((V7X_LANES - n_sel, tq), F32)], axis=0).astype(BF16)
    eye = (lax.broadcasted_iota(jnp.int32, (tq, tq), 0) == lax.broadcasted_iota(jnp.int32, (tq, tq), 1))
    sel_ref[0, 0] = _dot_nt(jnp.where(eye, 1.0, 0.0).astype(BF16), sel_t).astype(sel_ref.dtype)


def _compressed_attention(proj, kvc, batch, seq, groups, hpg, cols):
    tq = _tile(seq, 256)
    nq = seq // tq
    ncp = seq // NSA_CMP_STRIDE
    n_cmp = (seq - NSA_CMP_LEN) // NSA_CMP_STRIDE + 1
    n_sel = seq // NSA_SEL_BLOCK
    wq = hpg * HEAD_DIM
    q0 = cols["nq"] * 2 * V7X_LANES // wq
    kv_spec = lambda j: pl.BlockSpec((1, 1, 1, ncp, HEAD_DIM), lambda b, g, i: (j, b, g, 0, 0))
    return pl.pallas_call(
        functools.partial(_cmpattn_kernel, n_cmp=n_cmp, n_sel=n_sel, hpg=hpg),
        grid=(batch, groups, nq),
        in_specs=[
            pl.BlockSpec((tq, wq), lambda b, g, i: (b * nq + i, q0 + g)),
            kv_spec(0),
            kv_spec(1),
        ],
        out_specs=[
            pl.BlockSpec((tq, wq), lambda b, g, i: (b * nq + i, g)),
            pl.BlockSpec((1, 1, tq, V7X_LANES), lambda b, g, i: (b, g, i, 0)),
        ],
        out_shape=[
            jax.ShapeDtypeStruct((batch * seq, groups * wq), F32),
            jax.ShapeDtypeStruct((batch, groups, seq, V7X_LANES), BF16),
        ],
        scratch_shapes=[pltpu.VMEM((n_sel, tq), F32)],
        compiler_params=_cparams("arbitrary", "arbitrary", "arbitrary"),
        name="nsa_compressed_attention",
    )(proj, kvc, kvc)


def _sel_kernel(q_ref, ks_ref, vs_ref, kw_ref, vw_ref, sel_ref, oc_ref, gl_ref, o_ref,
                kaug_ref, m_ref, l_ref, acc_ref, s_ref, *, hpg, tk, n_sel, g_axis):
    i = pl.program_id(2)
    g = pl.program_id(g_axis)
    tq = q_ref.shape[0]
    seq = ks_ref.shape[0]
    rows = hpg * tq
    q0 = i * tq

    @pl.when(i == 0)
    def _():
        kaug_ref[:, :HEAD_DIM] = ks_ref[...]
        kpos = lax.broadcasted_iota(jnp.int32, (seq, V7X_LANES), 0)
        lane = lax.broadcasted_iota(jnp.int32, (seq, V7X_LANES), 1)
        kaug_ref[:, HEAD_DIM:] = jnp.where(_div_pow2(kpos, NSA_SEL_BLOCK) == lane, 1.0, 0.0).astype(BF16)

    q_heads = [q_ref[:, j * HEAD_DIM:(j + 1) * HEAD_DIM] for j in range(hpg)]
    q4 = jnp.concatenate(q_heads, axis=0)
    lane = lax.broadcasted_iota(jnp.int32, (tq, V7X_LANES), 1)
    unsel = (sel_ref[0, 0].astype(F32) < 0.5) & (lane < n_sel)
    block_bias = jnp.where(unsel, NEG, 0.0).astype(BF16)
    q_aug = jnp.concatenate([jnp.concatenate([qj, block_bias], axis=1) for qj in q_heads], axis=0)

    m_ref[...] = jnp.full_like(m_ref, NEG)
    l_ref[...] = jnp.zeros_like(l_ref)
    acc_ref[...] = jnp.zeros_like(acc_ref)

    def scores(j):
        return [_dot_nt(q_aug, kaug_ref[pl.ds(pl.multiple_of(j * tk, tk), tk), :])]

    def update(j, tiles):
        _softmax_update(tiles[0], vs_ref[pl.ds(pl.multiple_of(j * tk, tk), tk), :], m_ref, l_ref, acc_ref)

    n_full = _div_pow2(q0, tk)
    _flash_tiles(scores, update, n_full, _causal_mask(rows, tq, tk, q0, n_full * tk), s_ref)
    o_s = _softmax_finish(l_ref, acc_ref)

    span = NSA_WINDOW + tq
    wstart = pl.multiple_of(jnp.maximum(q0 - NSA_WINDOW, 0), tq)
    kt = kw_ref[pl.ds(wstart, span), :]
    vt = vw_ref[pl.ds(wstart, span), :]
    qpos = q0 + lax.broadcasted_iota(jnp.int32, (tq, span), 0)
    kcol = wstart + lax.broadcasted_iota(jnp.int32, (tq, span), 1)
    dist = qpos - kcol
    wbias = jnp.where((dist >= 0) & (dist < NSA_WINDOW), 0.0, NEG)
    s = _dot_nt(q4, kt) + jnp.concatenate([wbias] * hpg, axis=0)
    m = jnp.max(s, axis=-1, keepdims=True)
    p = jnp.exp(s - m)
    o_w = _dot(p.astype(BF16), vt) / jnp.sum(p, axis=-1, keepdims=True)

    gates = jax.nn.sigmoid(gl_ref[...])
    lane = lax.broadcasted_iota(jnp.int32, gates.shape, 1)
    for j in range(hpg):
        head = g * hpg + j
        rows = slice(j * tq, (j + 1) * tq)
        gsel = [jnp.sum(jnp.where(lane == 3 * head + c, gates, 0.0), axis=-1, keepdims=True) for c in range(3)]
        o = (gsel[0] * oc_ref[:, j * HEAD_DIM:(j + 1) * HEAD_DIM] + gsel[1] * o_s[rows] + gsel[2] * o_w[rows])
        o_ref[:, j * HEAD_DIM:(j + 1) * HEAD_DIM] = o.astype(o_ref.dtype)


def _selected_window_attention(proj, sel, o_c, gl, batch, seq, groups, hpg, cols):
    tq = _tile(seq, 128)
    nq = seq // tq
    n_sel = seq // NSA_SEL_BLOCK
    wq = hpg * HEAD_DIM
    q0 = cols["nq"] * 2 * V7X_LANES // wq
    kv_spec = lambda name, off: pl.BlockSpec(
        (seq, HEAD_DIM), lambda b, g, i: (b, (cols[name] * 2 + off * groups) + g))
    tk = _tile(seq, 1024)
    assert tk % tq == 0 and n_sel <= V7X_LANES and n_sel % 8 == 0
    return pl.pallas_call(
        functools.partial(_sel_kernel, hpg=hpg, tk=tk, n_sel=n_sel, g_axis=1),
        grid=(batch, groups, nq),
        in_specs=[
            pl.BlockSpec((tq, wq), lambda b, g, i: (b * nq + i, q0 + g)),
            kv_spec("ks", 0), kv_spec("ks", 1), kv_spec("kw", 0), kv_spec("kw", 1),
            pl.BlockSpec((1, 1, tq, V7X_LANES), lambda b, g, i: (b, g, i, 0)),
            pl.BlockSpec((tq, wq), lambda b, g, i: (b * nq + i, g)),
            pl.BlockSpec((tq, V7X_LANES), lambda b, g, i: (b * nq + i, 0)),
        ],
        out_specs=pl.BlockSpec((tq, wq), lambda b, g, i: (b * nq + i, g)),
        out_shape=jax.ShapeDtypeStruct((batch * seq, groups * wq), BF16),
        scratch_shapes=[pltpu.VMEM((seq, 2 * HEAD_DIM), BF16),
                        pltpu.VMEM((hpg * tq, 1), F32), pltpu.VMEM((hpg * tq, HEAD_DIM), F32),
                        pltpu.VMEM((hpg * tq, HEAD_DIM), F32),
                        pltpu.VMEM((2, 1, hpg * tq, tk), F32)],
        compiler_params=_cparams("arbitrary", "arbitrary", "arbitrary"),
        name="nsa_selected_window",
    )(proj, proj, proj, proj, proj, sel, o_c, gl)


def _out_kernel(od_ref, on_ref, w_ref, x_ref, mod_ref, o_ref, *, gate_idx, rc):
    kd = od_ref.shape[1]
    w = w_ref[...].astype(BF16)
    gate = mod_ref[0, gate_idx:gate_idx + 1, :]
    for r in range(od_ref.shape[0] // rc):
        rows = slice(r * rc, (r + 1) * rc)
        y = _dot(od_ref[rows, :], w[:kd]) + _dot(on_ref[rows, :], w[kd:])
        o_ref[rows, :] = x_ref[rows, :] + gate * y


def _output_projection(o_diff, o_nsa, w_o, l, x, mod_l, seq, gate_idx):
    t, d = x.shape
    kd, kn = o_diff.shape[1], o_nsa.shape[1]
    tm = _tile(seq, 1024)
    tn = _tile(d, 1024)
    per_b = seq // tm
    return pl.pallas_call(
        functools.partial(_out_kernel, gate_idx=gate_idx, rc=_tile(tm, 512)),
        grid=(t // tm, d // tn),
        in_specs=[
            pl.BlockSpec((tm, kd), lambda i, n: (i, 0)),
            pl.BlockSpec((tm, kn), lambda i, n: (i, 0)),
            pl.BlockSpec((None, kd + kn, tn), lambda i, n: (l, 0, n)),
            pl.BlockSpec((tm, tn), lambda i, n: (i, n)),
            pl.BlockSpec((1, N_MOD, tn), lambda i, n: (i // per_b, 0, n)),
        ],
        out_specs=pl.BlockSpec((tm, tn), lambda i, n: (i, n)),
        out_shape=jax.ShapeDtypeStruct((t, d), F32),
        compiler_params=_cparams("arbitrary", "arbitrary"),
        name="output_projection",
    )(o_diff, o_nsa, w_o, x, mod_l)


def kernel(x, c, w_ada, b_ada, norm_g, ffn_w_gu, ffn_w_d, w_in, w_o, diff_lam, diff_subln,
           cmp_pe, cmp_w1, cmp_w2, final_g):
    batch, seq, d = x.shape
    depth = w_ada.shape[0]
    n_diff = d // (2 * HEAD_DIM)
    n_nsa = d // (2 * HEAD_DIM)
    groups = N_NSA_KV
    hpg = n_nsa // groups
    diff_w = n_diff * HEAD_DIM
    nsa_w = n_nsa * HEAD_DIM
    kv_w = groups * HEAD_DIM
    n_main = 3 * diff_w + nsa_w + 6 * kv_w
    assert w_in.shape[-1] == n_main + 3 * n_nsa
    cw = 2 * V7X_LANES
    assert diff_w % cw == 0 and nsa_w % cw == 0 and kv_w == cw
    assert seq >= NSA_WINDOW + 128
    cols = {"dk": diff_w // cw, "dv": 2 * diff_w // cw, "nq": 3 * diff_w // cw}
    cols["kc"] = cols["nq"] + nsa_w // cw
    cols.update(vc=cols["kc"] + 1, ks=cols["kc"] + 2, vs=cols["kc"] + 3, kw=cols["kc"] + 4, vw=cols["kc"] + 5)

    t = batch * seq
    xf = x.reshape(t, d)
    c_pad = jnp.pad(c, ((0, 8 - batch), (0, 0)))
    mod = _modulation(c_pad, w_ada, b_ada).reshape(depth, 8, N_MOD, d)
    tabs = (_rope_tables(seq, DIFF_QK_DIM // ROPE_FRACTION, DIFF_QK_DIM)
            + _rope_tables(seq, HEAD_DIM // ROPE_FRACTION, HEAD_DIM))
    ncp = seq // NSA_CMP_STRIDE
    cmp_half = NSA_CMP_STRIDE * HEAD_DIM

    for l in range(depth):
        mod_l = mod[l]
        lam_init = 0.8 - 0.6 * math.exp(-0.3 * l)

        h = _norm(xf, norm_g[l, 0], mod_l, seq, 0, 1, BF16)
        a = _ffn_gate_up(h, ffn_w_gu, l, 0)
        xf = _gated_residual_matmul(a, ffn_w_d, (l, 0), xf, mod_l, seq, 2, 0.5)

        h = _norm(xf, norm_g[l, 1], mod_l, seq, 3, 4, BF16)
        w_gate = jnp.pad(w_in[l, :, n_main:], ((0, 0), (0, V7X_LANES - 3 * n_nsa)))
        proj, kvc_raw, gl = _input_projection(h, w_in, l, w_gate, tabs, seq, cols, n_main)
        o_diff = _diff_attention(proj, diff_lam[l], diff_subln[l], batch, seq, n_diff, cols, lam_init)
        xt = kvc_raw.reshape(2, batch, groups, ncp, cmp_half)
        pe2 = cmp_pe[l].reshape(2, 2, cmp_half)
        kvc = _compress(xt, pe2, cmp_w1[l], cmp_w2[l])
        o_c, sel = _compressed_attention(proj, kvc, batch, seq, groups, hpg, cols)
        o_nsa = _selected_window_attention(proj, sel, o_c, gl, batch, seq, groups, hpg, cols)
        xf = _output_projection(o_diff, o_nsa, w_o, l, xf, mod_l, seq, 5)

        h = _norm(xf, norm_g[l, 2], mod_l, seq, 6, 7, BF16)
        a = _ffn_gate_up(h, ffn_w_gu, l, 1)
        xf = _gated_residual_matmul(a, ffn_w_d, (l, 1), xf, mod_l, seq, 8, 0.5)

    out = _norm(xf, final_g, mod[0], seq, None, None, F32)
    return out.reshape(batch, seq, d)
```

```python
import functools
import math

import jax
import jax.numpy as jnp
from jax import lax
from jax.experimental import pallas as pl
from jax.experimental.pallas import tpu as pltpu

F32 = jnp.float32
BF16 = jnp.bfloat16

HEAD_DIM = 128
DIFF_QK_DIM = HEAD_DIM // 2
N_NSA_KV = 2
NSA_CMP_LEN = 32
NSA_CMP_STRIDE = 16
NSA_SEL_BLOCK = 64
NSA_TOP_N = 16
NSA_WINDOW = 512
ROPE_THETA = 500000.0
ROPE_FRACTION = 4
N_MOD = 9
RMS_EPS = 1e-6
NEG = -1e30
FORCE_BONUS = 1e6

V7X_LANES = 128
V7X_VMEM_LIMIT_BYTES = 56 * 1024 * 1024


def _cparams(*sem):
    return pltpu.CompilerParams(dimension_semantics=sem, vmem_limit_bytes=V7X_VMEM_LIMIT_BYTES)


def _tile(n, pref):
    t = min(pref, n)
    while n % t:
        t //= 2
    return t


def _dot(a, b):
    return jnp.dot(a, b, preferred_element_type=F32)


def _dot_nt(a, b):
    return lax.dot_general(a, b, (((1,), (1,)), ((), ())), preferred_element_type=F32)


def _silu(x):
    return x * jax.nn.sigmoid(x)


def _div_pow2(x, n):
    assert n & (n - 1) == 0
    return jnp.right_shift(x, n.bit_length() - 1)


def _mod_kernel(c_ref, w_ref, b_ref, o_ref):
    cs = _silu(c_ref[...]).astype(BF16)
    o_ref[0] = _dot(cs, w_ref[0].astype(BF16)) + b_ref[0]


def _modulation(c_pad, w_ada, b_ada):
    depth, d, n9 = w_ada.shape
    tn = _tile(n9, 1024)
    return pl.pallas_call(
        _mod_kernel,
        grid=(depth, n9 // tn),
        in_specs=[
            pl.BlockSpec((8, d), lambda l, n: (0, 0)),
            pl.BlockSpec((1, d, tn), lambda l, n: (l, 0, n)),
            pl.BlockSpec((1, 1, tn), lambda l, n: (l, 0, n)),
        ],
        out_specs=pl.BlockSpec((1, 8, tn), lambda l, n: (l, 0, n)),
        out_shape=jax.ShapeDtypeStruct((depth, 8, n9), F32),
        compiler_params=_cparams("arbitrary", "arbitrary"),
        name="adaln_mod",
    )(c_pad, w_ada, b_ada.reshape(depth, 1, n9))


def _norm_kernel(x_ref, g_ref, mod_ref, o_ref, *, shift_idx, scale_idx):
    x = x_ref[...]
    y = x * lax.rsqrt(jnp.mean(x * x, axis=-1, keepdims=True) + RMS_EPS) * g_ref[...]
    if scale_idx is not None:
        y = y * (1.0 + mod_ref[0, scale_idx:scale_idx + 1, :]) + mod_ref[0, shift_idx:shift_idx + 1, :]
    o_ref[...] = y.astype(o_ref.dtype)


def _norm(x, g, mod_l, seq, shift_idx, scale_idx, out_dtype):
    t, d = x.shape
    tr = _tile(seq, 512)
    per_b = seq // tr
    return pl.pallas_call(
        functools.partial(_norm_kernel, shift_idx=shift_idx, scale_idx=scale_idx),
        grid=(t // tr,),
        in_specs=[
            pl.BlockSpec((tr, d), lambda i: (i, 0)),
            pl.BlockSpec((1, d), lambda i: (0, 0)),
            pl.BlockSpec((1, N_MOD, d), lambda i: (i // per_b, 0, 0)),
        ],
        out_specs=pl.BlockSpec((tr, d), lambda i: (i, 0)),
        out_shape=jax.ShapeDtypeStruct((t, d), out_dtype),
        compiler_params=_cparams("arbitrary"),
        name="rms_adaln",
    )(x, g.reshape(1, d), mod_l)


def _gu_kernel(h_ref, wg_ref, wu_ref, o_ref, *, rc):
    wg = wg_ref[...].astype(BF16)
    wu = wu_ref[...].astype(BF16)
    for r in range(h_ref.shape[0] // rc):
        rows = slice(r * rc, (r + 1) * rc)
        h = h_ref[rows, :]
        g = _dot(h, wg)
        u = _dot(h, wu)
        o_ref[rows, :] = (_silu(g) * u).astype(BF16)


def _ffn_gate_up(h, w_gu, l, k):
    t, d = h.shape
    dff = w_gu.shape[-1] // 2
    tm = _tile(t, 2048)
    tf = _tile(dff, 256)
    nf = dff // tf
    return pl.pallas_call(
        functools.partial(_gu_kernel, rc=_tile(tm, 512)),
        grid=(t // tm, nf),
        in_specs=[
            pl.BlockSpec((tm, d), lambda i, f: (i, 0)),
            pl.BlockSpec((None, None, d, tf), lambda i, f: (l, k, 0, f)),
            pl.BlockSpec((None, None, d, tf), lambda i, f: (l, k, 0, f + nf)),
        ],
        out_specs=pl.BlockSpec((tm, tf), lambda i, f: (i, f)),
        out_shape=jax.ShapeDtypeStruct((t, dff), BF16),
        compiler_params=_cparams("arbitrary", "arbitrary"),
        name="ffn_gate_up",
    )(h, w_gu, w_gu)


def _down_kernel(a_ref, w_ref, x_ref, mod_ref, o_ref, *, gate_idx, coef, rc):
    k = pl.program_id(2)
    w = w_ref[...].astype(BF16)

    @pl.when(k == 0)
    def _():
        o_ref[...] = jnp.zeros_like(o_ref)

    for r in range(a_ref.shape[0] // rc):
        rows = slice(r * rc, (r + 1) * rc)
        o_ref[rows, :] += _dot(a_ref[rows, :], w)

    @pl.when(k == pl.num_programs(2) - 1)
    def _():
        gate = coef * mod_ref[0, gate_idx:gate_idx + 1, :]
        o_ref[...] = x_ref[...] + gate * o_ref[...]


def _gated_residual_matmul(a, w_full, w_prefix, x, mod_l, seq, gate_idx, coef):
    t, kdim = a.shape
    d = x.shape[1]
    tm = _tile(seq, 2048)
    tn = _tile(d, 1024)
    tk = _tile(kdim, 512)
    per_b = seq // tm
    npre = len(w_prefix)
    return pl.pallas_call(
        functools.partial(_down_kernel, gate_idx=gate_idx, coef=coef, rc=_tile(tm, 512)),
        grid=(t // tm, d // tn, kdim // tk),
        in_specs=[
            pl.BlockSpec((tm, tk), lambda i, n, k: (i, k)),
            pl.BlockSpec((None,) * npre + (tk, tn), lambda i, n, k: w_prefix + (k, n)),
            pl.BlockSpec((tm, tn), lambda i, n, k: (i, n)),
            pl.BlockSpec((1, N_MOD, tn), lambda i, n, k: (i // per_b, 0, n)),
        ],
        out_specs=pl.BlockSpec((tm, tn), lambda i, n, k: (i, n)),
        out_shape=jax.ShapeDtypeStruct((t, d), F32),
        compiler_params=_cparams("arbitrary", "arbitrary", "arbitrary"),
        name="gated_residual_matmul",
    )(a, w_full, x, mod_l)


def _rope_tile(x, cos, sin_signed, half, period):
    lane = lax.broadcasted_iota(jnp.int32, x.shape, 1)
    first = (lane & (period - 1)) < half
    partner = jnp.where(first, pltpu.roll(x, V7X_LANES - half, 1), pltpu.roll(x, half, 1))
    return x * cos + partner * sin_signed


def _in_kernel(h_ref, w_ref, wgate_ref, cd_ref, sd_ref, cn_ref, sn_ref,
               o_ref, okv_ref, og_ref, *, cols, rc, n_gate):
    n = pl.program_id(1)
    tm = h_ref.shape[0]
    w = w_ref[...].astype(BF16)
    chunks = [slice(r * rc, (r + 1) * rc) for r in range(tm // rc)]

    @pl.when(n == 0)
    def _():
        lane = lax.broadcasted_iota(jnp.int32, wgate_ref.shape, 1)
        wg = jnp.where(lane < n_gate, wgate_ref[...], 0.0).astype(BF16)
        for rows in chunks:
            og_ref[rows, :] = _dot(h_ref[rows, :], wg)

    is_dq = n < cols["dk"]
    is_diff = n < cols["dv"]
    is_nq = (n >= cols["nq"]) & (n < cols["kc"])
    is_nsa = is_nq | (n == cols["ks"]) | (n == cols["kw"])
    is_kvc = (n >= cols["kc"]) & (n < cols["ks"])

    def project(finish):
        for rows in chunks:
            finish(rows, _dot(h_ref[rows, :], w))

    def rope(cos_ref, sin_ref, half, period, scale):
        def finish(rows, y):
            c = cos_ref[rows, :]
            s = sin_ref[rows, :]
            halves = [_rope_tile(y[:, j * V7X_LANES:(j + 1) * V7X_LANES], c, s, half, period)
                      for j in range(y.shape[1] // V7X_LANES)]
            o_ref[rows, :] = (jnp.concatenate(halves, axis=1) * scale).astype(BF16)
        return finish

    @pl.when(is_diff)
    def _():
        project(rope(cd_ref, sd_ref, DIFF_QK_DIM // ROPE_FRACTION // 2, DIFF_QK_DIM,
                     jnp.where(is_dq, DIFF_QK_DIM ** -0.5, 1.0)))

    @pl.when(is_nsa)
    def _():
        project(rope(cn_ref, sn_ref, HEAD_DIM // ROPE_FRACTION // 2, HEAD_DIM,
                     jnp.where(is_nq, HEAD_DIM ** -0.5, 1.0)))

    @pl.when(is_kvc)
    def _():
        def finish(rows, y):
            o_ref[rows, :] = y.astype(BF16)
            for g in range(okv_ref.shape[2]):
                okv_ref[0, 0, g, rows, :] = y[:, g * HEAD_DIM:(g + 1) * HEAD_DIM]
        project(finish)

    @pl.when(jnp.logical_not(is_diff | is_nsa | is_kvc))
    def _():
        def finish(rows, y):
            o_ref[rows, :] = y.astype(BF16)
        project(finish)


def _input_projection(h, w_in, l, tabs, seq, cols, n_main):
    t, d = h.shape
    n_gate = w_in.shape[-1] - n_main
    assert n_main % V7X_LANES == 0 and 0 < n_gate <= V7X_LANES
    tm = _tile(seq, 2048)
    tn = 2 * V7X_LANES
    per_b = seq // tm
    nb = n_main // tn
    kc0 = cols["kc"]
    tab_spec = pl.BlockSpec((tm, V7X_LANES), lambda i, n: (i % per_b, 0))
    return pl.pallas_call(
        functools.partial(_in_kernel, cols=cols, rc=_tile(tm, 512), n_gate=n_gate),
        grid=(t // tm, nb),
        in_specs=[
            pl.BlockSpec((tm, d), lambda i, n: (i, 0)),
            pl.BlockSpec((None, d, tn), lambda i, n: (l, 0, n)),
            pl.BlockSpec((None, d, V7X_LANES), lambda i, n: (l, 0, n_main // V7X_LANES)),
            tab_spec, tab_spec, tab_spec, tab_spec,
        ],
        out_specs=[
            pl.BlockSpec((tm, tn), lambda i, n: (i, n)),
            pl.BlockSpec((1, 1, N_NSA_KV, tm, HEAD_DIM),
                         lambda i, n: (jnp.clip(n - kc0, 0, 1), i // per_b, 0, i % per_b, 0)),
            pl.BlockSpec((tm, V7X_LANES), lambda i, n: (i, 0)),
        ],
        out_shape=[
            jax.ShapeDtypeStruct((t, n_main), BF16),
            jax.ShapeDtypeStruct((2, t // seq, N_NSA_KV, seq, HEAD_DIM), F32),
            jax.ShapeDtypeStruct((t, V7X_LANES), F32),
        ],
        compiler_params=_cparams("arbitrary", "arbitrary"),
        name="input_projection",
    )(h, w_in, w_in, *tabs)


def _rope_tables(seq, rot_dim, period):
    half = rot_dim // 2
    inv = jnp.exp(-math.log(ROPE_THETA) * jnp.arange(half, dtype=F32) / half)
    ang = jnp.arange(seq, dtype=F32)[:, None] * inv[None, :]
    cos, sin = jnp.cos(ang), jnp.sin(ang)
    pad = jnp.zeros((seq, period - rot_dim), F32)
    cos_p = jnp.concatenate([cos, cos, pad + 1.0], axis=1)
    sin_p = jnp.concatenate([-sin, sin, pad], axis=1)
    reps = V7X_LANES // period
    return jnp.tile(cos_p, (1, reps)), jnp.tile(sin_p, (1, reps))


def _softmax_update(s, v_tile, m_ref, l_ref, acc_ref):
    m_old = m_ref[...]
    m_new = jnp.maximum(m_old, jnp.max(s, axis=-1, keepdims=True))
    alpha = jnp.exp(m_old - m_new)
    p = jnp.exp(s - m_new)
    if l_ref is not None:
        part = p[:, :V7X_LANES]
        for c in range(1, p.shape[1] // V7X_LANES):
            part = part + p[:, c * V7X_LANES:(c + 1) * V7X_LANES]
        l_ref[...] = alpha * l_ref[...] + part
    acc_ref[...] = alpha * acc_ref[...] + _dot(p.astype(BF16), v_tile)
    m_ref[...] = m_new


def _softmax_finish(l_ref, acc_ref):
    if l_ref is None:
        acc = acc_ref[...]
        return acc[:, :HEAD_DIM] / acc[:, HEAD_DIM:]
    return acc_ref[...] / jnp.sum(l_ref[...], axis=-1, keepdims=True)


def _flash_tiles(scores, update, n_full, diag_mask, s_ref):
    update(n_full, [jnp.where(diag_mask, s, NEG) for s in scores(n_full)])
    chains = s_ref.shape[1]

    def put(slot, tiles):
        for c, s in enumerate(tiles):
            s_ref[slot, c] = s

    def get(slot):
        return [s_ref[slot, c] for c in range(chains)]

    @pl.when(n_full > 0)
    def _():
        put(0, scores(0))

        def body(j, carry):
            cur = j & 1
            tiles = get(cur)
            put(1 - cur, scores(j + 1))
            update(j, tiles)
            return carry

        lax.fori_loop(0, n_full - 1, body, 0)
        last = n_full - 1
        update(last, get(last & 1))


def _causal_mask(rows, tq, tk, q0, k0):
    row = lax.broadcasted_iota(jnp.int32, (rows, tk), 0) & (tq - 1)
    col = lax.broadcasted_iota(jnp.int32, (rows, tk), 1)
    return k0 + col <= q0 + row


def _diff_kernel(lam_ref, sg_ref, q_ref, k_ref, v_ref, o_ref, vext_ref, m_ref, acc_ref, s_ref,
                 *, lam_init, tk, heads):
    i = pl.program_id(2)
    tq = q_ref.shape[0]
    seq = k_ref.shape[0]
    rows = 2 * tq

    @pl.when(i == 0)
    def _():
        for hh in range(heads):
            vext_ref[hh, :, :HEAD_DIM] = v_ref[:, hh * HEAD_DIM:(hh + 1) * HEAD_DIM]
            vext_ref[hh, :, HEAD_DIM:] = jnp.ones((seq, HEAD_DIM), BF16)

    q2 = []
    for hh in range(heads):
        q = q_ref[:, hh * HEAD_DIM:(hh + 1) * HEAD_DIM]
        lane = lax.broadcasted_iota(jnp.int32, q.shape, 1)
        zero = jnp.zeros_like(q)
        q2.append(jnp.concatenate([jnp.where(lane < DIFF_QK_DIM, q, zero),
                                   jnp.where(lane >= DIFF_QK_DIM, q, zero)], axis=0))
    m_ref[...] = jnp.full_like(m_ref, NEG)
    acc_ref[...] = jnp.zeros_like(acc_ref)

    def scores(j):
        start = pl.multiple_of(j * tk, tk)
        return [_dot_nt(q2[hh], k_ref[pl.ds(start, tk), hh * HEAD_DIM:(hh + 1) * HEAD_DIM])
                for hh in range(heads)]

    def update(j, tiles):
        start = pl.multiple_of(j * tk, tk)
        for hh in range(heads):
            _softmax_update(tiles[hh], vext_ref[hh, pl.ds(start, tk), :], m_ref.at[hh], None, acc_ref.at[hh])

    n_full = _div_pow2(i * tq, tk)
    _flash_tiles(scores, update, n_full, _causal_mask(rows, tq, tk, i * tq, n_full * tk), s_ref)

    lam_p = lam_ref[...]
    lam = (jnp.exp(jnp.sum(lam_p[0:1] * lam_p[1:2], axis=-1, keepdims=True))
           - jnp.exp(jnp.sum(lam_p[2:3] * lam_p[3:4], axis=-1, keepdims=True)) + lam_init)
    for hh in range(heads):
        o_all = _softmax_finish(None, acc_ref.at[hh])
        o = o_all[:tq] - lam * o_all[tq:]
        o = o * lax.rsqrt(jnp.mean(o * o, axis=-1, keepdims=True) + RMS_EPS) * sg_ref[...]
        o_ref[:, hh * HEAD_DIM:(hh + 1) * HEAD_DIM] = (o * (1.0 - lam_init)).astype(o_ref.dtype)


def _diff_attention(proj, lam_p, subln_g, batch, seq, n_heads, cols, lam_init):
    tq = _tile(seq, 256)
    tk = _tile(seq, 1024)
    heads = 2
    assert n_heads % heads == 0 and tk % tq == 0
    nq = seq // tq
    wide = heads * HEAD_DIM
    k0 = cols["dk"] * 2 // heads
    v0 = cols["dv"] * 2 // heads
    kv_spec = lambda c0: pl.BlockSpec((seq, wide), lambda b, h, i: (b, c0 + h))
    return pl.pallas_call(
        functools.partial(_diff_kernel, lam_init=lam_init, tk=tk, heads=heads),
        grid=(batch, n_heads // heads, nq),
        in_specs=[
            pl.BlockSpec((4, DIFF_QK_DIM), lambda b, h, i: (0, 0)),
            pl.BlockSpec((1, HEAD_DIM), lambda b, h, i: (0, 0)),
            pl.BlockSpec((tq, wide), lambda b, h, i: (b * nq + i, h)),
            kv_spec(k0),
            kv_spec(v0),
        ],
        out_specs=pl.BlockSpec((tq, wide), lambda b, h, i: (b * nq + i, h)),
        out_shape=jax.ShapeDtypeStruct((batch * seq, n_heads * HEAD_DIM), BF16),
        scratch_shapes=[pltpu.VMEM((heads, seq, 2 * HEAD_DIM), BF16),
                        pltpu.VMEM((heads, 2 * tq, 1), F32),
                        pltpu.VMEM((heads, 2 * tq, 2 * HEAD_DIM), F32),
                        pltpu.VMEM((2, heads, 2 * tq, tk), F32)],
        compiler_params=_cparams("arbitrary", "arbitrary", "arbitrary"),
        name="diff_attention",
    )(lam_p, subln_g.reshape(1, HEAD_DIM), proj, proj, proj)


def _cmp_kernel(x_ref, pe_ref, w1_ref, w2_ref, o_ref):
    nrow = o_ref.shape[3]
    hidden = w1_ref.shape[2]
    ya = jnp.zeros((nrow, hidden), F32)
    yb = jnp.zeros((nrow, hidden), F32)
    for r in range(NSA_CMP_STRIDE):
        xr = x_ref[0, 0, 0, pl.ds(r, nrow, stride=NSA_CMP_STRIDE), :]
        lo = slice(r * HEAD_DIM, (r + 1) * HEAD_DIM)
        hi = slice((NSA_CMP_STRIDE + r) * HEAD_DIM, (NSA_CMP_STRIDE + r + 1) * HEAD_DIM)
        ya = ya + _dot((xr + pe_ref[0, r:r + 1, :]).astype(BF16), w1_ref[0, lo, :].astype(BF16))
        yb = yb + _dot((xr + pe_ref[0, NSA_CMP_STRIDE + r:NSA_CMP_STRIDE + r + 1, :]).astype(BF16),
                       w1_ref[0, hi, :].astype(BF16))
    row = lax.broadcasted_iota(jnp.int32, yb.shape, 0)
    yb_next = jnp.where(row < nrow - 1, pltpu.roll(yb, nrow - 1, 0), 0.0)
    hid = _silu(ya + yb_next).astype(BF16)
    o_ref[0, 0, 0] = _dot(hid, w2_ref[0].astype(BF16))


def _compress(kv_raw, pe, w1, w2):
    _, batch, groups, seq, _ = kv_raw.shape
    nrow = seq // NSA_CMP_STRIDE
    hidden = w1.shape[-1]
    return pl.pallas_call(
        _cmp_kernel,
        grid=(2, batch, groups),
        in_specs=[
            pl.BlockSpec((1, 1, 1, seq, HEAD_DIM), lambda j, b, g: (j, b, g, 0, 0)),
            pl.BlockSpec((1, NSA_CMP_LEN, HEAD_DIM), lambda j, b, g: (j, 0, 0)),
            pl.BlockSpec((1, NSA_CMP_LEN * HEAD_DIM, hidden), lambda j, b, g: (j, 0, 0)),
            pl.BlockSpec((1, hidden, HEAD_DIM), lambda j, b, g: (j, 0, 0)),
        ],
        out_specs=pl.BlockSpec((1, 1, 1, nrow, HEAD_DIM), lambda j, b, g: (j, b, g, 0, 0)),
        out_shape=jax.ShapeDtypeStruct((2, batch, groups, nrow, HEAD_DIM), F32),
        compiler_params=_cparams("arbitrary", "arbitrary", "arbitrary"),
        name="nsa_compress",
    )(kv_raw, pe, w1, w2)


def _cmpattn_kernel(q_ref, kc_ref, vc_ref, oc_ref, sel_ref, score_ref, *, n_cmp, n_sel, hpg):
    i = pl.program_id(2)
    tq = q_ref.shape[0]
    kc = kc_ref[0, 0, 0].astype(BF16)
    vc = vc_ref[0, 0, 0].astype(BF16)
    ncp = kc.shape[0]
    tpos = i * tq + lax.broadcasted_iota(jnp.int32, (tq, ncp), 0)
    nidx = lax.broadcasted_iota(jnp.int32, (tq, ncp), 1)
    cmask = (nidx * NSA_CMP_STRIDE + NSA_CMP_LEN - 1 <= tpos) & (nidx < n_cmp)
    psum = jnp.zeros((tq, ncp), F32)
    for j in range(hpg):
        qj = q_ref[:, j * HEAD_DIM:(j + 1) * HEAD_DIM]
        s = jnp.where(cmask, _dot_nt(qj, kc), NEG)
        m = jnp.max(s, axis=-1, keepdims=True)
        e = jnp.where(cmask, jnp.exp(s - m), 0.0)
        den = jnp.sum(e, axis=-1, keepdims=True)
        p = e / jnp.where(den > 0.0, den, 1.0)
        oc_ref[:, j * HEAD_DIM:(j + 1) * HEAD_DIM] = _dot(p.astype(BF16), vc)
        psum = psum + p

    sm = lax.broadcasted_iota(jnp.int32, (n_sel, ncp), 0) * NSA_SEL_BLOCK
    cn = lax.broadcasted_iota(jnp.int32, (n_sel, ncp), 1) * NSA_CMP_STRIDE
    ov = jnp.maximum(jnp.minimum(cn + NSA_CMP_LEN, sm + NSA_SEL_BLOCK) - jnp.maximum(cn, sm), 0)
    ov = (ov.astype(F32) / NSA_CMP_LEN).astype(BF16)
    p_hi = psum.astype(BF16)
    p_lo = (psum - p_hi.astype(F32)).astype(BF16)
    imp = _dot_nt(ov, p_hi) + _dot_nt(ov, p_lo)

    t = i * tq + lax.broadcasted_iota(jnp.int32, (n_sel, tq), 1)
    blk = lax.broadcasted_iota(jnp.int32, (n_sel, tq), 0)
    cur = _div_pow2(t, NSA_SEL_BLOCK)
    valid = blk * NSA_SEL_BLOCK <= t
    forced = (blk == 0) | (blk == cur) | (blk == cur - 1)
    score = jnp.where(valid, imp + jnp.where(forced, FORCE_BONUS, 0.0), NEG)
    score_ref[...] = score
    rank = jnp.zeros((n_sel, tq), F32)
    for c in range(n_sel):
        other = score_ref[c:c + 1, :]
        ahead = (other > score) | ((other == score) & (blk > c))
        rank = rank + jnp.where(ahead, 1.0, 0.0)
    n_top = min(NSA_TOP_N, n_sel)
    sel_t = jnp.where((rank < n_top) & valid, 1.0, 0.0)
    sel_t = jnp.concatenate([sel_t, jnp.zeros((V7X_LANES - n_sel, tq), F32)], axis=0).astype(BF16)
    eye = (lax.broadcasted_iota(jnp.int32, (tq, tq), 0) == lax.broadcasted_iota(jnp.int32, (tq, tq), 1))
    sel_ref[0, 0] = _dot_nt(jnp.where(eye, 1.0, 0.0).astype(BF16), sel_t).astype(sel_ref.dtype)


def _compressed_attention(proj, kvc, batch, seq, groups, hpg, cols):
    tq = _tile(seq, 256)
    nq = seq // tq
    ncp = seq // NSA_CMP_STRIDE
    n_cmp = (seq - NSA_CMP_LEN) // NSA_CMP_STRIDE + 1
    n_sel = seq // NSA_SEL_BLOCK
    wq = hpg * HEAD_DIM
    q0 = cols["nq"] * 2 * V7X_LANES // wq
    kv_spec = lambda j: pl.BlockSpec((1, 1, 1, ncp, HEAD_DIM), lambda b, g, i: (j, b, g, 0, 0))
    return pl.pallas_call(
        functools.partial(_cmpattn_kernel, n_cmp=n_cmp, n_sel=n_sel, hpg=hpg),
        grid=(batch, groups, nq),
        in_specs=[
            pl.BlockSpec((tq, wq), lambda b, g, i: (b * nq + i, q0 + g)),
            kv_spec(0),
            kv_spec(1),
        ],
        out_specs=[
            pl.BlockSpec((tq, wq), lambda b, g, i: (b * nq + i, g)),
            pl.BlockSpec((1, 1, tq, V7X_LANES), lambda b, g, i: (b, g, i, 0)),
        ],
        out_shape=[
            jax.ShapeDtypeStruct((batch * seq, groups * wq), F32),
            jax.ShapeDtypeStruct((batch, groups, seq, V7X_LANES), BF16),
        ],
        scratch_shapes=[pltpu.VMEM((n_sel, tq), F32)],
        compiler_params=_cparams("arbitrary", "arbitrary", "arbitrary"),
        name="nsa_compressed_attention",
    )(proj, kvc, kvc)


def _sel_kernel(q_ref, ks_ref, vs_ref, kw_ref, vw_ref, sel_ref, oc_ref, gl_ref, o_ref,
                kaug_ref, m_ref, l_ref, acc_ref, s_ref, *, hpg, tk, n_sel, g_axis):
    i = pl.program_id(2)
    g = pl.program_id(g_axis)
    tq = q_ref.shape[0]
    seq = ks_ref.shape[0]
    rows = hpg * tq
    q0 = i * tq

    @pl.when(i == 0)
    def _():
        kaug_ref[:, :HEAD_DIM] = ks_ref[...]
        kpos = lax.broadcasted_iota(jnp.int32, (seq, V7X_LANES), 0)
        lane = lax.broadcasted_iota(jnp.int32, (seq, V7X_LANES), 1)
        kaug_ref[:, HEAD_DIM:] = jnp.where(_div_pow2(kpos, NSA_SEL_BLOCK) == lane, 1.0, 0.0).astype(BF16)

    q_heads = [q_ref[:, j * HEAD_DIM:(j + 1) * HEAD_DIM] for j in range(hpg)]
    q4 = jnp.concatenate(q_heads, axis=0)
    lane = lax.broadcasted_iota(jnp.int32, (tq, V7X_LANES), 1)
    unsel = (sel_ref[0, 0].astype(F32) < 0.5) & (lane < n_sel)
    block_bias = jnp.where(unsel, NEG, 0.0).astype(BF16)
    q_aug = jnp.concatenate([jnp.concatenate([qj, block_bias], axis=1) for qj in q_heads], axis=0)

    m_ref[...] = jnp.full_like(m_ref, NEG)
    l_ref[...] = jnp.zeros_like(l_ref)
    acc_ref[...] = jnp.zeros_like(acc_ref)

    def scores(j):
        return [_dot_nt(q_aug, kaug_ref[pl.ds(pl.multiple_of(j * tk, tk), tk), :])]

    def update(j, tiles):
        _softmax_update(tiles[0], vs_ref[pl.ds(pl.multiple_of(j * tk, tk), tk), :], m_ref, l_ref, acc_ref)

    n_full = _div_pow2(q0, tk)
    _flash_tiles(scores, update, n_full, _causal_mask(rows, tq, tk, q0, n_full * tk), s_ref)
    o_s = _softmax_finish(l_ref, acc_ref)

    span = NSA_WINDOW + tq
    wstart = pl.multiple_of(jnp.maximum(q0 - NSA_WINDOW, 0), tq)
    kt = kw_ref[pl.ds(wstart, span), :]
    vt = vw_ref[pl.ds(wstart, span), :]
    qpos = q0 + lax.broadcasted_iota(jnp.int32, (tq, span), 0)
    kcol = wstart + lax.broadcasted_iota(jnp.int32, (tq, span), 1)
    dist = qpos - kcol
    wbias = jnp.where((dist >= 0) & (dist < NSA_WINDOW), 0.0, NEG)
    s = _dot_nt(q4, kt) + jnp.concatenate([wbias] * hpg, axis=0)
    m = jnp.max(s, axis=-1, keepdims=True)
    p = jnp.exp(s - m)
    o_w = _dot(p.astype(BF16), vt) / jnp.sum(p, axis=-1, keepdims=True)

    gates = jax.nn.sigmoid(gl_ref[...])
    lane = lax.broadcasted_iota(jnp.int32, gates.shape, 1)
    for j in range(hpg):
        head = g * hpg + j
        rows = slice(j * tq, (j + 1) * tq)
        gsel = [jnp.sum(jnp.where(lane == 3 * head + c, gates, 0.0), axis=-1, keepdims=True) for c in range(3)]
        o = (gsel[0] * oc_ref[:, j * HEAD_DIM:(j + 1) * HEAD_DIM] + gsel[1] * o_s[rows] + gsel[2] * o_w[rows])
        o_ref[:, j * HEAD_DIM:(j + 1) * HEAD_DIM] = o.astype(o_ref.dtype)


def _selected_window_attention(proj, sel, o_c, gl, batch, seq, groups, hpg, cols):
    tq = _tile(seq, 128)
    nq = seq // tq
    n_sel = seq // NSA_SEL_BLOCK
    wq = hpg * HEAD_DIM
    q0 = cols["nq"] * 2 * V7X_LANES // wq
    kv_spec = lambda name, off: pl.BlockSpec(
        (seq, HEAD_DIM), lambda b, g, i: (b, (cols[name] * 2 + off * groups) + g))
    tk = _tile(seq, 1024)
    assert tk % tq == 0 and n_sel <= V7X_LANES and n_sel % 8 == 0
    return pl.pallas_call(
        functools.partial(_sel_kernel, hpg=hpg, tk=tk, n_sel=n_sel, g_axis=1),
        grid=(batch, groups, nq),
        in_specs=[
            pl.BlockSpec((tq, wq), lambda b, g, i: (b * nq + i, q0 + g)),
            kv_spec("ks", 0), kv_spec("ks", 1), kv_spec("kw", 0), kv_spec("kw", 1),
            pl.BlockSpec((1, 1, tq, V7X_LANES), lambda b, g, i: (b, g, i, 0)),
            pl.BlockSpec((tq, wq), lambda b, g, i: (b * nq + i, g)),
            pl.BlockSpec((tq, V7X_LANES), lambda b, g, i: (b * nq + i, 0)),
        ],
        out_specs=pl.BlockSpec((tq, wq), lambda b, g, i: (b * nq + i, g)),
        out_shape=jax.ShapeDtypeStruct((batch * seq, groups * wq), BF16),
        scratch_shapes=[pltpu.VMEM((seq, 2 * HEAD_DIM), BF16),
                        pltpu.VMEM((hpg * tq, 1), F32), pltpu.VMEM((hpg * tq, HEAD_DIM), F32),
                        pltpu.VMEM((hpg * tq, HEAD_DIM), F32),
                        pltpu.VMEM((2, 1, hpg * tq, tk), F32)],
        compiler_params=_cparams("arbitrary", "arbitrary", "arbitrary"),
        name="nsa_selected_window",
    )(proj, proj, proj, proj, proj, sel, o_c, gl)


def _out_kernel(od_ref, on_ref, w_ref, x_ref, mod_ref, o_ref, *, gate_idx, rc):
    kd = od_ref.shape[1]
    w = w_ref[...].astype(BF16)
    gate = mod_ref[0, gate_idx:gate_idx + 1, :]
    for r in range(od_ref.shape[0] // rc):
        rows = slice(r * rc, (r + 1) * rc)
        y = _dot(od_ref[rows, :], w[:kd]) + _dot(on_ref[rows, :], w[kd:])
        o_ref[rows, :] = x_ref[rows, :] + gate * y


def _output_projection(o_diff, o_nsa, w_o, l, x, mod_l, seq, gate_idx):
    t, d = x.shape
    kd, kn = o_diff.shape[1], o_nsa.shape[1]
    tm = _tile(seq, 2048)
    tn = _tile(d, 512)
    per_b = seq // tm
    return pl.pallas_call(
        functools.partial(_out_kernel, gate_idx=gate_idx, rc=_tile(tm, 512)),
        grid=(t // tm, d // tn),
        in_specs=[
            pl.BlockSpec((tm, kd), lambda i, n: (i, 0)),
            pl.BlockSpec((tm, kn), lambda i, n: (i, 0)),
            pl.BlockSpec((None, kd + kn, tn), lambda i, n: (l, 0, n)),
            pl.BlockSpec((tm, tn), lambda i, n: (i, n)),
            pl.BlockSpec((1, N_MOD, tn), lambda i, n: (i // per_b, 0, n)),
        ],
        out_specs=pl.BlockSpec((tm, tn), lambda i, n: (i, n)),
        out_shape=jax.ShapeDtypeStruct((t, d), F32),
        compiler_params=_cparams("arbitrary", "arbitrary"),
        name="output_projection",
    )(o_diff, o_nsa, w_o, x, mod_l)


def kernel(x, c, w_ada, b_ada, norm_g, ffn_w_gu, ffn_w_d, w_in, w_o, diff_lam, diff_subln,
           cmp_pe, cmp_w1, cmp_w2, final_g):
    batch, seq, d = x.shape
    depth = w_ada.shape[0]
    n_diff = d // (2 * HEAD_DIM)
    n_nsa = d // (2 * HEAD_DIM)
    groups = N_NSA_KV
    hpg = n_nsa // groups
    diff_w = n_diff * HEAD_DIM
    nsa_w = n_nsa * HEAD_DIM
    kv_w = groups * HEAD_DIM
    n_main = 3 * diff_w + nsa_w + 6 * kv_w
    assert w_in.shape[-1] == n_main + 3 * n_nsa
    cw = 2 * V7X_LANES
    assert diff_w % cw == 0 and nsa_w % cw == 0 and kv_w == cw
    assert seq >= NSA_WINDOW + 128
    cols = {"dk": diff_w // cw, "dv": 2 * diff_w // cw, "nq": 3 * diff_w // cw}
    cols["kc"] = cols["nq"] + nsa_w // cw
    cols.update(vc=cols["kc"] + 1, ks=cols["kc"] + 2, vs=cols["kc"] + 3, kw=cols["kc"] + 4, vw=cols["kc"] + 5)

    t = batch * seq
    xf = x.reshape(t, d)
    c_pad = jnp.pad(c, ((0, 8 - batch), (0, 0)))
    mod = _modulation(c_pad, w_ada, b_ada).reshape(depth, 8, N_MOD, d)
    tabs = (_rope_tables(seq, DIFF_QK_DIM // ROPE_FRACTION, DIFF_QK_DIM)
            + _rope_tables(seq, HEAD_DIM // ROPE_FRACTION, HEAD_DIM))
    ncp = seq // NSA_CMP_STRIDE
    cmp_half = NSA_CMP_STRIDE * HEAD_DIM

    for l in range(depth):
        mod_l = mod[l]
        lam_init = 0.8 - 0.6 * math.exp(-0.3 * l)

        h = _norm(xf, norm_g[l, 0], mod_l, seq, 0, 1, BF16)
        a = _ffn_gate_up(h, ffn_w_gu, l, 0)
        xf = _gated_residual_matmul(a, ffn_w_d, (l, 0), xf, mod_l, seq, 2, 0.5)

        h = _norm(xf, norm_g[l, 1], mod_l, seq, 3, 4, BF16)
        proj, kvc_raw, gl = _input_projection(h, w_in, l, tabs, seq, cols, n_main)
        o_diff = _diff_attention(proj, diff_lam[l], diff_subln[l], batch, seq, n_diff, cols, lam_init)
        kvc = _compress(kvc_raw, cmp_pe[l], cmp_w1[l], cmp_w2[l])
        o_c, sel = _compressed_attention(proj, kvc, batch, seq, groups, hpg, cols)
        o_nsa = _selected_window_attention(proj, sel, o_c, gl, batch, seq, groups, hpg, cols)
        xf = _output_projection(o_diff, o_nsa, w_o, l, xf, mod_l, seq, 5)

        h = _norm(xf, norm_g[l, 2], mod_l, seq, 6, 7, BF16)
        a = _ffn_gate_up(h, ffn_w_gu, l, 1)
        xf = _gated_residual_matmul(a, ffn_w_d, (l, 1), xf, mod_l, seq, 8, 0.5)

    out = _norm(xf, final_g, mod[0], seq, None, None, F32)
    return out.reshape(batch, seq, d)
```
